```python
import math
import jax, jax.numpy as jnp
from jax import lax
import numpy as np

D_MODEL = 2048
BATCH = 1
SEQ = 8192
DEPTH = 4

C_A = 512
H_B = 16
HEAD_DIM = 64
C_B = H_B * HEAD_DIM
C_C = 512
POOL_WINDOWS = (2, 4, 8, 16)
N_POOL_GROUPS = len(POOL_WINDOWS)
C_G = C_C // N_POOL_GROUPS
MIX_WIDTH = C_A + C_B + C_C
IN_WIDTH = 2 * C_A + 3 * C_B + C_C

CONV_WIDTH = 31
CONV_HALF = CONV_WIDTH // 2

DILATED_PATTERNS = ((128, 1), (512, 4), (2048, 16))
ATTN_BLOCK = 64
ROT_DIM = HEAD_DIM // 4
ROPE_THETA = 500000.0

FFN_HIDDEN = int(math.ceil((8 * D_MODEL / 3) / 256) * 256)
EPS = 1e-6
NEG = -1e30

kernel_name = "hybrid_conv_dilatedattn_pool_encoder"


def rms_normalize(t):
    tf = t.astype(jnp.float32)
    return tf * lax.rsqrt(jnp.mean(tf * tf, axis=-1, keepdims=True) + EPS)


def rmsnorm(t, g):
    return (rms_normalize(t) * g.astype(jnp.float32)).astype(t.dtype)


def rope_tables(S):
    pos = jnp.arange(S, dtype=jnp.float32)
    inv = ROPE_THETA ** (-jnp.arange(0, ROT_DIM, 2, dtype=jnp.float32) / ROT_DIM)
    ang = pos[:, None] * inv[None, :]
    return jnp.cos(ang), jnp.sin(ang)


def apply_partial_rope(t, cos, sin):
    half = ROT_DIM // 2
    t1 = t[..., :half]
    t2 = t[..., half:ROT_DIM]
    c = cos[None, :, None, :]
    s = sin[None, :, None, :]
    return jnp.concatenate([t1 * c - t2 * s, t2 * c + t1 * s, t[..., ROT_DIM:]], axis=-1)


def conformer_conv(u, w, b, ln_g, ln_b):
    a, gate = jnp.split(u, 2, axis=-1)
    h = a * jax.nn.sigmoid(gate)
    h = lax.conv_general_dilated(
        h, w[:, None, :], window_strides=(1,),
        padding=[(CONV_HALF, CONV_HALF)],
        dimension_numbers=('NWC', 'WIO', 'NWC'),
        feature_group_count=C_A) + b
    hf = h.astype(jnp.float32)
    mu = jnp.mean(hf, axis=-1, keepdims=True)
    var = jnp.mean(jnp.square(hf - mu), axis=-1, keepdims=True)
    hf = (hf - mu) * lax.rsqrt(var + EPS) * ln_g.astype(jnp.float32) + ln_b.astype(jnp.float32)
    return jax.nn.silu(hf).astype(u.dtype)


def banded_dilated_stats(q, k, v, dilation, half):
    B, S, H, E = q.shape
    L = S // dilation
    nb = -(-L // ATTN_BLOCK)
    Lp = nb * ATTN_BLOCK
    qd = q.reshape(B, L, dilation, H, E)
    kd = k.reshape(B, L, dilation, H, E)
    vd = v.reshape(B, L, dilation, H, E)
    pad_q = ((0, 0), (0, Lp - L), (0, 0), (0, 0), (0, 0))
    pad_kv = ((0, 0), (ATTN_BLOCK, Lp - L + ATTN_BLOCK), (0, 0), (0, 0), (0, 0))
    qb = jnp.pad(qd, pad_q).reshape(B, nb, ATTN_BLOCK, dilation, H, E)
    kb = jnp.pad(kd, pad_kv).reshape(B, nb + 2, ATTN_BLOCK, dilation, H, E)
    vb = jnp.pad(vd, pad_kv).reshape(B, nb + 2, ATTN_BLOCK, dilation, H, E)
    kwin = jnp.concatenate([kb[:, :-2], kb[:, 1:-1], kb[:, 2:]], axis=2)
    vwin = jnp.concatenate([vb[:, :-2], vb[:, 1:-1], vb[:, 2:]], axis=2)
    blk = jnp.arange(nb)[:, None]
    jq = blk * ATTN_BLOCK + jnp.arange(ATTN_BLOCK)[None, :]
    jk = (blk - 1) * ATTN_BLOCK + jnp.arange(3 * ATTN_BLOCK)[None, :]
    mask = ((jnp.abs(jk[:, None, :] - jq[:, :, None]) <= half)
            & (jk[:, None, :] >= 0) & (jk[:, None, :] < L))
    scores = jnp.einsum('bnqrhe,bnkrhe->bnrhqk', qb, kwin)
    maskb = mask[None, :, None, None, :, :]
    scores = jnp.where(maskb, scores, NEG)
    m = jnp.max(scores, axis=-1)
    p = jnp.where(maskb, jnp.exp(scores - m[..., None]), 0.0)
    s = jnp.sum(p, axis=-1)
    o = jnp.einsum('bnrhqk,bnkrhe->bnqrhe', p, vwin)
    m = jnp.transpose(m, (0, 1, 4, 2, 3)).reshape(B, Lp, dilation, H)[:, :L].reshape(B, S, H)
    s = jnp.transpose(s, (0, 1, 4, 2, 3)).reshape(B, Lp, dilation, H)[:, :L].reshape(B, S, H)
    o = o.reshape(B, Lp, dilation, H, E)[:, :L].reshape(B, S, H, E)
    return m, s, o


def dilated_attention(qkv, cos, sin):
    B, S, _ = qkv.shape
    q, k, v = jnp.split(qkv.astype(jnp.float32), 3, axis=-1)
    q = apply_partial_rope(q.reshape(B, S, H_B, HEAD_DIM), cos, sin) * (HEAD_DIM ** -0.5)
    k = apply_partial_rope(k.reshape(B, S, H_B, HEAD_DIM), cos, sin)
    v = v.reshape(B, S, H_B, HEAD_DIM)
    ms, ss, os_ = [], [], []
    for window, dilation in DILATED_PATTERNS:
        m, s, o = banded_dilated_stats(q, k, v, dilation, window // (2 * dilation))
        ms.append(m); ss.append(s); os_.append(o)
    m_all = jnp.stack(ms)
    s_all = jnp.stack(ss)
    o_all = jnp.stack(os_)
    wgt = jnp.exp(m_all - jnp.max(m_all, axis=0, keepdims=True))
    num = jnp.sum(wgt[..., None] * o_all, axis=0)
    den = jnp.sum(wgt * s_all, axis=0)
    out = num / den[..., None]
    return out.reshape(B, S, C_B).astype(qkv.dtype)


def pool_mixer(u, w, scale):
    B, S, _ = u.shape
    uf = u.astype(jnp.float32)
    cs = jnp.concatenate([jnp.zeros((B, 1, C_C), jnp.float32), jnp.cumsum(uf, axis=1)], axis=1)
    pos = jnp.arange(S)
    outs = []
    for gi, win in enumerate(POOL_WINDOWS):
        seg = cs[..., gi * C_G:(gi + 1) * C_G]
        lo = jnp.clip(pos - win // 2, 0, S)
        hi = jnp.clip(pos + win - win // 2, 0, S)
        mean = (seg[:, hi] - seg[:, lo]) / (hi - lo).astype(jnp.float32)[None, :, None]
        outs.append(mean - uf[..., gi * C_G:(gi + 1) * C_G])
    pooled = jnp.stack(outs, axis=2)
    mixed = jnp.einsum('bsgc,gcd->bsgd', pooled, w.astype(jnp.float32))
    return (mixed.reshape(B, S, C_C) * scale.astype(jnp.float32)).astype(u.dtype)


def setup_inputs(seed: int = 0) -> dict:
    key = jax.random.key(seed)
    ks = jax.random.split(key, 20)
    f32 = jnp.float32

    def nrm(k, shape, scale):
        return jax.random.normal(k, shape, f32) * scale

    def gain(k, shape):
        return 1.0 + 0.05 * jax.random.normal(k, shape, f32)

    return {
        "x": jax.random.normal(ks[0], (BATCH, SEQ, D_MODEL), f32),
        "w_in": nrm(ks[1], (DEPTH, D_MODEL, IN_WIDTH), D_MODEL ** -0.5),
        "conv_w": nrm(ks[2], (DEPTH, CONV_WIDTH, C_A), CONV_WIDTH ** -0.5),
        "conv_b": nrm(ks[3], (DEPTH, C_A), 0.02),
        "conv_ln_g": gain(ks[4], (DEPTH, C_A)),
        "conv_ln_b": nrm(ks[5], (DEPTH, C_A), 0.02),
        "pool_w": nrm(ks[6], (DEPTH, N_POOL_GROUPS, C_G, C_G), C_G ** -0.5),
        "pool_scale": gain(ks[7], (DEPTH, C_C)),
        "g_mix": gain(ks[8], (DEPTH, MIX_WIDTH)),
        "w_out": nrm(ks[9], (DEPTH, MIX_WIDTH, D_MODEL), MIX_WIDTH ** -0.5),
        "g_pre_mix": gain(ks[10], (DEPTH, D_MODEL)),
        "g_post_mix": gain(ks[11], (DEPTH, D_MODEL)),
        "g_pre_ffn": gain(ks[12], (DEPTH, D_MODEL)),
        "g_post_ffn": gain(ks[13], (DEPTH, D_MODEL)),
        "w_gate": nrm(ks[14], (DEPTH, D_MODEL, FFN_HIDDEN), D_MODEL ** -0.5),
        "w_up": nrm(ks[15], (DEPTH, D_MODEL, FFN_HIDDEN), D_MODEL ** -0.5),
        "w_down": nrm(ks[16], (DEPTH, FFN_HIDDEN, D_MODEL), FFN_HIDDEN ** -0.5),
    }


def reference(x, w_in, conv_w, conv_b, conv_ln_g, conv_ln_b, pool_w, pool_scale, g_mix,
              w_out, g_pre_mix, g_post_mix, g_pre_ffn, g_post_ffn, w_gate, w_up, w_down):
    S = x.shape[1]
    cos, sin = rope_tables(S)
    a_end = 2 * C_A
    b_end = a_end + 3 * C_B
    for l in range(DEPTH):
        h = rmsnorm(x, g_pre_mix[l])
        proj = h @ w_in[l]
        y_a = conformer_conv(proj[..., :a_end], conv_w[l], conv_b[l], conv_ln_g[l], conv_ln_b[l])
        y_b = dilated_attention(proj[..., a_end:b_end], cos, sin)
        y_c = pool_mixer(proj[..., b_end:], pool_w[l], pool_scale[l])
        y = jnp.concatenate([rms_normalize(y_a), rms_normalize(y_b), rms_normalize(y_c)], axis=-1)
        y = (y * g_mix[l].astype(jnp.float32)).astype(x.dtype)
        x = x + rmsnorm(y @ w_out[l], g_post_mix[l])
        h = rmsnorm(x, g_pre_ffn[l])
        f = (jax.nn.silu(h @ w_gate[l]) * (h @ w_up[l])) @ w_down[l]
        x = x + rmsnorm(f, g_post_ffn[l])
    return x
```

```python
import functools

import numpy as np
import jax
import jax.numpy as jnp
from jax import lax
from jax.experimental import pallas as pl
from jax.experimental.pallas import tpu as pltpu

F32 = jnp.float32
BF16 = jnp.bfloat16

D_MODEL = 2048
C_A = 512
N_HEADS = 16
HEAD_DIM = 64
C_B = N_HEADS * HEAD_DIM
C_C = 512
POOL_WINDOWS = (2, 4, 8, 16)
C_G = C_C // len(POOL_WINDOWS)
MIX_WIDTH = C_A + C_B + C_C
IN_WIDTH = 2 * C_A + 3 * C_B + C_C
CONV_WIDTH = 31
CONV_HALF = CONV_WIDTH // 2
DILATIONS = (1, 4, 16)
HALF_WINDOW = 64
ROT_DIM = HEAD_DIM // 4
ROPE_THETA = 500000.0
FFN_HIDDEN = 5632
EPS = 1e-6
NEG = -1e30

LANES = 128
N_PAIRS = N_HEADS // 2
HALO = 16
VMEM_LIMIT = 52 * 1024 * 1024


def _rms_scale(t):
    return lax.rsqrt(jnp.mean(t * t, axis=-1, keepdims=True) + EPS)


IN_TN = 512
IN_TM = 512


def _in_proj_kernel(x_ref, g_ref, w_ref, cos_ref, sin_lo_ref, sin_hi_ref,
                    conv_ref, qkv_ref, pool_ref, h_ref):
    j = pl.program_id(1)

    @pl.when(j == 0)
    def _():
        x = x_ref[...]
        h_ref[...] = (x * _rms_scale(x) * g_ref[...]).astype(BF16)

    acc = jnp.dot(h_ref[...], w_ref[...], preferred_element_type=F32)
    n_slabs = IN_TN // LANES

    @pl.when(j < 2)
    def _():
        conv_ref[...] = acc

    @pl.when((j >= 2) & (j < 6))
    def _():
        scale = jnp.where(j < 4, HEAD_DIM ** -0.5, 1.0).astype(F32)
        for t in range(n_slabs):
            blk = acc[:, t * LANES:(t + 1) * LANES]
            rot = (blk * cos_ref[...]
                   + pltpu.roll(blk, ROT_DIM // 2, 1) * sin_hi_ref[...]
                   + pltpu.roll(blk, LANES - ROT_DIM // 2, 1) * sin_lo_ref[...])
            qkv_ref[t] = rot * scale

    @pl.when((j >= 6) & (j < 8))
    def _():
        for t in range(n_slabs):
            qkv_ref[t] = acc[:, t * LANES:(t + 1) * LANES]

    @pl.when(j == 8)
    def _():
        pool_ref[...] = acc


def _in_proj(x, g, w, cos_t, sin_lo_t, sin_hi_t):
    S = x.shape[0]
    tm = min(IN_TM, S)
    n_slabs = IN_TN // LANES
    grid = (S // tm, IN_WIDTH // IN_TN)
    return pl.pallas_call(
        _in_proj_kernel,
        grid=grid,
        in_specs=[
            pl.BlockSpec((tm, D_MODEL), lambda i, j: (i, 0)),
            pl.BlockSpec((1, D_MODEL), lambda i, j: (0, 0)),
            pl.BlockSpec((D_MODEL, IN_TN), lambda i, j: (0, j)),
            pl.BlockSpec((tm, LANES), lambda i, j: (i, 0)),
            pl.BlockSpec((tm, LANES), lambda i, j: (i, 0)),
            pl.BlockSpec((tm, LANES), lambda i, j: (i, 0)),
        ],
        out_specs=[
            pl.BlockSpec((tm, IN_TN), lambda i, j: (i, jnp.minimum(j, 1))),
            pl.BlockSpec((n_slabs, tm, LANES), lambda i, j: (jnp.clip(j - 2, 0, 5), i, 0)),
            pl.BlockSpec((tm, IN_TN), lambda i, j: (i, 0)),
        ],
        out_shape=[
            jax.ShapeDtypeStruct((S, 2 * C_A), F32),
            jax.ShapeDtypeStruct((3 * N_PAIRS, S, LANES), F32),
            jax.ShapeDtypeStruct((S, C_C), F32),
        ],
        scratch_shapes=[pltpu.VMEM((tm, D_MODEL), BF16)],
        compiler_params=pltpu.CompilerParams(
            dimension_semantics=("arbitrary", "arbitrary"), vmem_limit_bytes=VMEM_LIMIT),
        name="in_proj",
    )(x, g, w, cos_t, sin_lo_t, sin_hi_t)


MIX_TS = 512
ROW_CHUNK = 128


def _local_mix_kernel(cu_ref, cu_prev_ref, cu_next_ref, pu_ref, pu_prev_ref, pu_next_ref,
                      cw_ref, cb_ref, lng_ref, lnb_ref, pw_ref, ps_ref, ga_ref, gc_ref,
                      ya_ref, yc_ref, hbuf, cbuf, ubuf, *, seq_len):
    i = pl.program_id(0)
    n = pl.num_programs(0)
    ts = ya_ref.shape[0]
    not_first = (i > 0).astype(F32)
    not_last = (i < n - 1).astype(F32)

    def glu(u):
        return u[:, :C_A] * jax.nn.sigmoid(u[:, C_A:])

    hbuf[0:HALO, :] = glu(cu_prev_ref[...]) * not_first
    hbuf[HALO:HALO + ts, :] = glu(cu_ref[...])
    hbuf[HALO + ts:, :] = glu(cu_next_ref[...]) * not_last

    tap0 = HALO - CONV_HALF
    for rc in range(ts // ROW_CHUNK):
        for lc in range(C_A // LANES):
            cols = slice(lc * LANES, (lc + 1) * LANES)
            acc = jnp.zeros((ROW_CHUNK, LANES), F32)
            for t in range(CONV_WIDTH):
                rows = slice(rc * ROW_CHUNK + tap0 + t, rc * ROW_CHUNK + tap0 + t + ROW_CHUNK)
                acc = acc + hbuf[rows, cols] * cw_ref[t:t + 1, cols]
            cbuf[rc * ROW_CHUNK:(rc + 1) * ROW_CHUNK, cols] = acc + cb_ref[:, cols]

    for rc in range(ts // ROW_CHUNK):
        rows = slice(rc * ROW_CHUNK, (rc + 1) * ROW_CHUNK)
        c = cbuf[rows, :]
        mu = jnp.mean(c, axis=-1, keepdims=True)
        cc = c - mu
        var = jnp.mean(cc * cc, axis=-1, keepdims=True)
        y = cc * lax.rsqrt(var + EPS) * lng_ref[...] + lnb_ref[...]
        y = y * jax.nn.sigmoid(y)
        ya_ref[rows, :] = (y * _rms_scale(y) * ga_ref[...]).astype(ya_ref.dtype)

    ubuf[0:HALO, :] = pu_prev_ref[...] * not_first
    ubuf[HALO:HALO + ts, :] = pu_ref[...]
    ubuf[HALO + ts:, :] = pu_next_ref[...] * not_last

    for rc in range(ts // ROW_CHUNK):
        r0 = rc * ROW_CHUNK
        pos = i * ts + r0 + lax.broadcasted_iota(jnp.int32, (ROW_CHUNK, LANES), 0)
        mixed = []
        for gi, win in enumerate(POOL_WINDOWS):
            cols = slice(gi * C_G, (gi + 1) * C_G)
            half = win // 2
            tot = jnp.zeros((ROW_CHUNK, LANES), F32)
            for o in range(-half, win - half):
                tot = tot + ubuf[HALO + r0 + o:HALO + r0 + o + ROW_CHUNK, cols]
            lo = jnp.maximum(pos - half, 0)
            hi = jnp.minimum(pos + (win - half), seq_len)
            pooled = tot / (hi - lo).astype(F32) - ubuf[HALO + r0:HALO + r0 + ROW_CHUNK, cols]
            m = jnp.dot(pooled.astype(BF16), pw_ref[gi], preferred_element_type=F32)
            mixed.append(m * ps_ref[:, cols])
        ssq = sum(jnp.sum(m * m, axis=-1, keepdims=True) for m in mixed)
        scale = lax.rsqrt(ssq * (1.0 / C_C) + EPS)
        for gi, m in enumerate(mixed):
            cols = slice(gi * C_G, (gi + 1) * C_G)
            yc_ref[r0:r0 + ROW_CHUNK, cols] = (m * scale * gc_ref[:, cols]).astype(yc_ref.dtype)


def _local_mix(conv_u, pool_u, conv_w, conv_b, ln_g, ln_b, pool_w, pool_scale, g_a, g_c):
    S = conv_u.shape[0]
    ts = min(MIX_TS, S)
    hb = ts // HALO
    last_hb = S // HALO - 1
    prev_map = lambda i: (jnp.maximum(i * hb - 1, 0), 0)
    next_map = lambda i: (jnp.minimum((i + 1) * hb, last_hb), 0)
    row = lambda i: (i, 0)
    const2 = lambda i: (0, 0)
    return pl.pallas_call(
        functools.partial(_local_mix_kernel, seq_len=S),
        grid=(S // ts,),
        in_specs=[
            pl.BlockSpec((ts, 2 * C_A), row),
            pl.BlockSpec((HALO, 2 * C_A), prev_map),
            pl.BlockSpec((HALO, 2 * C_A), next_map),
            pl.BlockSpec((ts, C_C), row),
            pl.BlockSpec((HALO, C_C), prev_map),
            pl.BlockSpec((HALO, C_C), next_map),
            pl.BlockSpec((CONV_WIDTH, C_A), const2),
            pl.BlockSpec((1, C_A), const2),
            pl.BlockSpec((1, C_A), const2),
            pl.BlockSpec((1, C_A), const2),
            pl.BlockSpec((len(POOL_WINDOWS), C_G, C_G), lambda i: (0, 0, 0)),
            pl.BlockSpec((1, C_C), const2),
            pl.BlockSpec((1, C_A), const2),
            pl.BlockSpec((1, C_C), const2),
        ],
        out_specs=[pl.BlockSpec((ts, C_A), row), pl.BlockSpec((ts, C_C), row)],
        out_shape=[jax.ShapeDtypeStruct((S, C_A), BF16), jax.ShapeDtypeStruct((S, C_C), BF16)],
        scratch_shapes=[
            pltpu.VMEM((ts + 2 * HALO, C_A), F32),
            pltpu.VMEM((ts, C_A), F32),
            pltpu.VMEM((ts + 2 * HALO, C_C), F32),
        ],
        compiler_params=pltpu.CompilerParams(
            dimension_semantics=("arbitrary",), vmem_limit_bytes=VMEM_LIMIT),
        name="local_mix",
    )(conv_u, conv_u, conv_u, pool_u, pool_u, pool_u,
      conv_w, conv_b, ln_g, ln_b, pool_w, pool_scale, g_a, g_c)


ATT_TQ = 128
ATT_W = ATT_TQ + 2 * HALF_WINDOW
COPY_CHUNK = 256


def _band_bias():
    r = np.arange(ATT_TQ)[:, None]
    c = np.arange(ATT_W)[None, :]
    band = (c - r >= 0) & (c - r <= 2 * HALF_WINDOW)
    first = band & (c >= HALF_WINDOW)
    last = band & (c < ATT_TQ + HALF_WINDOW)
    return np.where(np.stack([band, first, last]), 0.0, NEG).astype(np.float32)


def _attention_kernel(q_ref, k_ref, v_ref, bias_ref, o_ref, qd, kd, vd, m_acc, s_acc):
    S = o_ref.shape[0]
    lane = lax.broadcasted_iota(jnp.int32, (ATT_TQ, LANES), 1)
    head_a = lane < HEAD_DIM
    nt_dims = (((1,), (1,)), ((), ()))

    for pattern, d in enumerate(DILATIONS):
        L = S // d
        LP = L + 2 * HALF_WINDOW
        n_blk = L // ATT_TQ
        chunk = min(COPY_CHUNK, L)

        def copy_residue(r, carry):
            zpad = jnp.zeros((HALF_WINDOW, LANES), BF16)
            kbase = pl.multiple_of(r * LP, HALF_WINDOW)
            for buf in (kd, vd):
                buf[pl.ds(kbase, HALF_WINDOW), :] = zpad
                buf[pl.ds(kbase + HALF_WINDOW + L, HALF_WINDOW), :] = zpad

            def copy_chunk(c, carry2):
                src = pl.ds(r + d * c * chunk, chunk, stride=d)
                qdst = pl.multiple_of(r * L + c * chunk, chunk)
                kdst = pl.multiple_of(r * LP + HALF_WINDOW + c * chunk, HALF_WINDOW)
                qd[pl.ds(qdst, chunk), :] = q_ref[0, src, :].astype(BF16)
                kd[pl.ds(kdst, chunk), :] = k_ref[0, src, :].astype(BF16)
                vd[pl.ds(kdst, chunk), :] = v_ref[0, src, :].astype(BF16)
                return carry2

            return lax.fori_loop(0, L // chunk, copy_chunk, carry)

        lax.fori_loop(0, d, copy_residue, 0)

        def residue_blocks(r, carry):
            def block(jb, carry2):
                j0 = jb * ATT_TQ
                q2 = qd[pl.ds(pl.multiple_of(r * L + j0, ATT_TQ), ATT_TQ), :]
                kv0 = pl.multiple_of(r * LP + j0, HALF_WINDOW)
                kw = kd[pl.ds(kv0, ATT_W), :]
                vw = vd[pl.ds(kv0, ATT_W), :]
                sel = jnp.where(jb == 0, 1, jnp.where(jb == n_blk - 1, 2, 0))
                bias = bias_ref[sel]
                zero = jnp.zeros_like(q2)

                def one_head(qh):
                    s = lax.dot_general(qh, kw, nt_dims, preferred_element_type=F32) + bias
                    m = jnp.max(s, axis=-1, keepdims=True)
                    p = jnp.exp(s - m)
                    l = jnp.sum(p, axis=-1, keepdims=True)
                    o = jnp.dot(p.astype(BF16), vw, preferred_element_type=F32)
                    return m, l, o

                ma, la, oa = one_head(jnp.where(head_a, q2, zero))
                mb, lb, ob = one_head(jnp.where(head_a, zero, q2))
                m_blk = jnp.where(head_a, ma, mb)
                l_blk = jnp.where(head_a, la, lb)
                o_blk = jnp.where(head_a, oa, ob)

                rows = pl.ds(r + d * j0, ATT_TQ, stride=d)
                if pattern == 0:
                    m_acc[rows, :] = m_blk
                    s_acc[rows, :] = l_blk
                    o_ref[rows, :] = o_blk
                else:
                    m_old = m_acc[rows, :]
                    m_new = jnp.maximum(m_old, m_blk)
                    w_old = jnp.exp(m_old - m_new)
                    w_blk = jnp.exp(m_blk - m_new)
                    m_acc[rows, :] = m_new
                    s_acc[rows, :] = w_old * s_acc[rows, :] + w_blk * l_blk
                    o_ref[rows, :] = w_old * o_ref[rows, :] + w_blk * o_blk
                return carry2

            return lax.fori_loop(0, n_blk, block, carry)

        lax.fori_loop(0, d, residue_blocks, 0)

    def normalise(c, carry):
        rows = pl.ds(pl.multiple_of(c * COPY_CHUNK, COPY_CHUNK), COPY_CHUNK)
        o_ref[rows, :] = o_ref[rows, :] / s_acc[rows, :]
        return carry

    lax.fori_loop(0, S // COPY_CHUNK, normalise, 0)


def _attention(qkv, bias):
    S = qkv.shape[1]
    assert S % (DILATIONS[-1] * ATT_TQ * 2) == 0, "first and last key blocks must be distinct"
    pad_rows = DILATIONS[-1] * 2 * HALF_WINDOW
    qkv_spec = lambda off: pl.BlockSpec((1, S, LANES), lambda p: (p + off, 0, 0),
                                        pipeline_mode=pl.Buffered(1))
    return pl.pallas_call(
        _attention_kernel,
        grid=(N_PAIRS,),
        in_specs=[
            qkv_spec(0), qkv_spec(N_PAIRS), qkv_spec(2 * N_PAIRS),
            pl.BlockSpec((3, ATT_TQ, ATT_W), lambda p: (0, 0, 0)),
        ],
        out_specs=pl.BlockSpec((S, LANES), lambda p: (0, p)),
        out_shape=jax.ShapeDtypeStruct((S, C_B), F32),
        scratch_shapes=[
            pltpu.VMEM((S, LANES), BF16),
            pltpu.VMEM((S + pad_rows, LANES), BF16),
            pltpu.VMEM((S + pad_rows, LANES), BF16),
            pltpu.VMEM((S, LANES), F32),
            pltpu.VMEM((S, LANES), F32),
        ],
        compiler_params=pltpu.CompilerParams(
            dimension_semantics=("arbitrary",), vmem_limit_bytes=VMEM_LIMIT),
        name="attention",
    )(qkv, qkv, qkv, bias)


OUT_TM = 512


def _out_proj_kernel(ya_ref, yb_ref, yc_ref, gb_ref, w_ref, x_ref, gp_ref, o_ref):
    yb = yb_ref[...]
    ybn = (yb * _rms_scale(yb) * gb_ref[...]).astype(BF16)
    acc = jnp.dot(ya_ref[...], w_ref[0:C_A, :], preferred_element_type=F32)
    acc = acc + jnp.dot(ybn, w_ref[C_A:C_A + C_B, :], preferred_element_type=F32)
    acc = acc + jnp.dot(yc_ref[...], w_ref[C_A + C_B:, :], preferred_element_type=F32)
    o_ref[...] = x_ref[...] + acc * _rms_scale(acc) * gp_ref[...]


def _out_proj(ya, yb, yc, g_b, w, x, g_post):
    S = x.shape[0]
    tm = min(OUT_TM, S)
    row = lambda i: (i, 0)
    const2 = lambda i: (0, 0)
    return pl.pallas_call(
        _out_proj_kernel,
        grid=(S // tm,),
        in_specs=[
            pl.BlockSpec((tm, C_A), row),
            pl.BlockSpec((tm, C_B), row),
            pl.BlockSpec((tm, C_C), row),
            pl.BlockSpec((1, C_B), const2),
            pl.BlockSpec((MIX_WIDTH, D_MODEL), const2),
            pl.BlockSpec((tm, D_MODEL), row),
            pl.BlockSpec((1, D_MODEL), const2),
        ],
        out_specs=pl.BlockSpec((tm, D_MODEL), row),
        out_shape=jax.ShapeDtypeStruct((S, D_MODEL), F32),
        compiler_params=pltpu.CompilerParams(
            dimension_semantics=("arbitrary",), vmem_limit_bytes=VMEM_LIMIT),
        name="out_proj",
    )(ya, yb, yc, g_b, w, x, g_post)


FFN_TM = 512
FFN_TH = 512


def _ffn_kernel(x_ref, gpre_ref, wg_ref, wu_ref, wd_ref, gpost_ref, o_ref, h_ref, acc_ref):
    j = pl.program_id(1)

    @pl.when(j == 0)
    def _():
        x = x_ref[...]
        h_ref[...] = (x * _rms_scale(x) * gpre_ref[...]).astype(BF16)
        acc_ref[...] = jnp.zeros_like(acc_ref)

    h = h_ref[...]
    gate = jnp.dot(h, wg_ref[...], preferred_element_type=F32)
    up = jnp.dot(h, wu_ref[...], preferred_element_type=F32)
    act = (gate * jax.nn.sigmoid(gate) * up).astype(BF16)
    acc_ref[...] += jnp.dot(act, wd_ref[...], preferred_element_type=F32)

    @pl.when(j == pl.num_programs(1) - 1)
    def _():
        f = acc_ref[...]
        o_ref[...] = x_ref[...] + f * _rms_scale(f) * gpost_ref[...]


def _ffn(x, g_pre, w_gate, w_up, w_down, g_post):
    S = x.shape[0]
    tm = min(FFN_TM, S)
    return pl.pallas_call(
        _ffn_kernel,
        grid=(S // tm, FFN_HIDDEN // FFN_TH),
        in_specs=[
            pl.BlockSpec((tm, D_MODEL), lambda i, j: (i, 0)),
            pl.BlockSpec((1, D_MODEL), lambda i, j: (0, 0)),
            pl.BlockSpec((D_MODEL, FFN_TH), lambda i, j: (0, j)),
            pl.BlockSpec((D_MODEL, FFN_TH), lambda i, j: (0, j)),
            pl.BlockSpec((FFN_TH, D_MODEL), lambda i, j: (j, 0)),
            pl.BlockSpec((1, D_MODEL), lambda i, j: (0, 0)),
        ],
        out_specs=pl.BlockSpec((tm, D_MODEL), lambda i, j: (i, 0)),
        out_shape=jax.ShapeDtypeStruct((S, D_MODEL), F32),
        scratch_shapes=[pltpu.VMEM((tm, D_MODEL), BF16), pltpu.VMEM((tm, D_MODEL), F32)],
        compiler_params=pltpu.CompilerParams(
            dimension_semantics=("arbitrary", "arbitrary"), vmem_limit_bytes=VMEM_LIMIT),
        name="ffn",
    )(x, g_pre, w_gate, w_up, w_down, g_post)


def _rope_lane_tables(S):
    half = ROT_DIM // 2
    pos = jnp.arange(S, dtype=F32)
    inv = ROPE_THETA ** (-jnp.arange(0, ROT_DIM, 2, dtype=F32) / ROT_DIM)
    ang = pos[:, None] * inv[None, :]
    cos, sin = jnp.cos(ang), jnp.sin(ang)
    ones = jnp.ones((S, HEAD_DIM - ROT_DIM), F32)
    zeros_h = jnp.zeros((S, half), F32)
    zeros_r = jnp.zeros((S, HEAD_DIM - ROT_DIM), F32)
    cos_h = jnp.concatenate([cos, cos, ones], axis=1)
    sin_lo_h = jnp.concatenate([-sin, zeros_h, zeros_r], axis=1)
    sin_hi_h = jnp.concatenate([zeros_h, sin, zeros_r], axis=1)
    pair = lambda t: jnp.concatenate([t, t], axis=1)
    return pair(cos_h), pair(sin_lo_h), pair(sin_hi_h)


def kernel(x, w_in, conv_w, conv_b, conv_ln_g, conv_ln_b, pool_w, pool_scale, g_mix, w_out,
           g_pre_mix, g_post_mix, g_pre_ffn, g_post_ffn, w_gate, w_up, w_down):
    B, S, _ = x.shape
    depth = w_in.shape[0]
    cos_t, sin_lo_t, sin_hi_t = _rope_lane_tables(S)
    bias = jnp.asarray(_band_bias())
    row = lambda v: v.reshape(1, -1)
    w_in, w_out, w_gate, w_up, w_down, pool_w = (
        t.astype(BF16) for t in (w_in, w_out, w_gate, w_up, w_down, pool_w))

    outs = []
    for b in range(B):
        xb = x[b]
        for l in range(depth):
            conv_u, qkv, pool_u = _in_proj(xb, row(g_pre_mix[l]), w_in[l], cos_t, sin_lo_t, sin_hi_t)
            ya, yc = _local_mix(conv_u, pool_u, conv_w[l], row(conv_b[l]), row(conv_ln_g[l]),
                                row(conv_ln_b[l]), pool_w[l], row(pool_scale[l]),
                                row(g_mix[l, :C_A]), row(g_mix[l, C_A + C_B:]))
            yb = _attention(qkv, bias)
            xb = _out_proj(ya, yb, yc, row(g_mix[l, C_A:C_A + C_B]), w_out[l], xb, row(g_post_mix[l]))
            xb = _ffn(xb, row(g_pre_ffn[l]), w_gate[l], w_up[l], w_down[l], row(g_post_ffn[l]))
        outs.append(xb)
    return jnp.stack(outs)
```

```python
import functools

import numpy as np
import jax
import jax.numpy as jnp
from jax import lax
from jax.experimental import pallas as pl
from jax.experimental.pallas import tpu as pltpu

F32 = jnp.float32
BF16 = jnp.bfloat16

D_MODEL = 2048
C_A = 512
N_HEADS = 16
HEAD_DIM = 64
C_B = N_HEADS * HEAD_DIM
C_C = 512
POOL_WINDOWS = (2, 4, 8, 16)
C_G = C_C // len(POOL_WINDOWS)
MIX_WIDTH = C_A + C_B + C_C
IN_WIDTH = 2 * C_A + 3 * C_B + C_C
CONV_WIDTH = 31
CONV_HALF = CONV_WIDTH // 2
DILATIONS = (1, 4, 16)
HALF_WINDOW = 64
ROT_DIM = HEAD_DIM // 4
ROPE_THETA = 500000.0
FFN_HIDDEN = 5632
EPS = 1e-6
NEG = -1e30

LANES = 128
N_PAIRS = N_HEADS // 2
HALO = 16
VMEM_LIMIT = 52 * 1024 * 1024


def _rms_scale(t):
    return lax.rsqrt(jnp.mean(t * t, axis=-1, keepdims=True) + EPS)


IN_TN = 512
IN_TM = 512


def _in_proj_kernel(x_ref, g_ref, w_ref, cos_ref, sin_lo_ref, sin_hi_ref,
                    conv_ref, qkv_ref, pool_ref, h_ref):
    j = pl.program_id(1)

    @pl.when(j == 0)
    def _():
        x = x_ref[...]
        h_ref[...] = (x * _rms_scale(x) * g_ref[...]).astype(BF16)

    acc = jnp.dot(h_ref[...], w_ref[...], preferred_element_type=F32)
    n_slabs = IN_TN // LANES

    @pl.when(j < 2)
    def _():
        conv_ref[...] = acc

    @pl.when((j >= 2) & (j < 6))
    def _():
        scale = jnp.where(j < 4, HEAD_DIM ** -0.5, 1.0).astype(F32)
        for t in range(n_slabs):
            blk = acc[:, t * LANES:(t + 1) * LANES]
            rot = (blk * cos_ref[...]
                   + pltpu.roll(blk, ROT_DIM // 2, 1) * sin_hi_ref[...]
                   + pltpu.roll(blk, LANES - ROT_DIM // 2, 1) * sin_lo_ref[...])
            qkv_ref[t] = rot * scale

    @pl.when((j >= 6) & (j < 8))
    def _():
        for t in range(n_slabs):
            qkv_ref[t] = acc[:, t * LANES:(t + 1) * LANES]

    @pl.when(j == 8)
    def _():
        pool_ref[...] = acc


def _in_proj(x, g, w, layer, cos_t, sin_lo_t, sin_hi_t):
    S = x.shape[0]
    tm = min(IN_TM, S)
    n_slabs = IN_TN // LANES
    grid = (S // tm, IN_WIDTH // IN_TN)
    return pl.pallas_call(
        _in_proj_kernel,
        grid=grid,
        in_specs=[
            pl.BlockSpec((tm, D_MODEL), lambda i, j: (i, 0)),
            pl.BlockSpec((1, D_MODEL), lambda i, j: (0, 0)),
            pl.BlockSpec((None, D_MODEL, IN_TN), lambda i, j: (layer, 0, j)),
            pl.BlockSpec((tm, LANES), lambda i, j: (i, 0)),
            pl.BlockSpec((tm, LANES), lambda i, j: (i, 0)),
            pl.BlockSpec((tm, LANES), lambda i, j: (i, 0)),
        ],
        out_specs=[
            pl.BlockSpec((tm, IN_TN), lambda i, j: (i, jnp.minimum(j, 1))),
            pl.BlockSpec((n_slabs, tm, LANES), lambda i, j: (jnp.clip(j - 2, 0, 5), i, 0)),
            pl.BlockSpec((tm, IN_TN), lambda i, j: (i, 0)),
        ],
        out_shape=[
            jax.ShapeDtypeStruct((S, 2 * C_A), F32),
            jax.ShapeDtypeStruct((3 * N_PAIRS, S, LANES), F32),
            jax.ShapeDtypeStruct((S, C_C), F32),
        ],
        scratch_shapes=[pltpu.VMEM((tm, D_MODEL), BF16)],
        compiler_params=pltpu.CompilerParams(
            dimension_semantics=("arbitrary", "arbitrary"), vmem_limit_bytes=VMEM_LIMIT),
        name="in_proj",
    )(x, g, w, cos_t, sin_lo_t, sin_hi_t)


MIX_TS = 512
ROW_CHUNK = 128


def _local_mix_kernel(cu_ref, cu_prev_ref, cu_next_ref, pu_ref, pu_prev_ref, pu_next_ref,
                      cw_ref, cb_ref, lng_ref, lnb_ref, pw_ref, ps_ref, ga_ref, gc_ref,
                      ya_ref, yc_ref, hbuf, cbuf, ubuf, *, seq_len):
    i = pl.program_id(0)
    n = pl.num_programs(0)
    ts = ya_ref.shape[0]
    not_first = (i > 0).astype(F32)
    not_last = (i < n - 1).astype(F32)

    def glu(u):
        return u[:, :C_A] * jax.nn.sigmoid(u[:, C_A:])

    hbuf[0:HALO, :] = glu(cu_prev_ref[...]) * not_first
    hbuf[HALO:HALO + ts, :] = glu(cu_ref[...])
    hbuf[HALO + ts:, :] = glu(cu_next_ref[...]) * not_last

    tap0 = HALO - CONV_HALF
    for rc in range(ts // ROW_CHUNK):
        for lc in range(C_A // LANES):
            cols = slice(lc * LANES, (lc + 1) * LANES)
            acc = jnp.zeros((ROW_CHUNK, LANES), F32)
            for t in range(CONV_WIDTH):
                rows = slice(rc * ROW_CHUNK + tap0 + t, rc * ROW_CHUNK + tap0 + t + ROW_CHUNK)
                acc = acc + hbuf[rows, cols] * cw_ref[t:t + 1, cols]
            cbuf[rc * ROW_CHUNK:(rc + 1) * ROW_CHUNK, cols] = acc + cb_ref[:, cols]

    for rc in range(ts // ROW_CHUNK):
        rows = slice(rc * ROW_CHUNK, (rc + 1) * ROW_CHUNK)
        c = cbuf[rows, :]
        mu = jnp.mean(c, axis=-1, keepdims=True)
        cc = c - mu
        var = jnp.mean(cc * cc, axis=-1, keepdims=True)
        y = cc * lax.rsqrt(var + EPS) * lng_ref[...] + lnb_ref[...]
        y = y * jax.nn.sigmoid(y)
        ya_ref[rows, :] = (y * _rms_scale(y) * ga_ref[...]).astype(ya_ref.dtype)

    ubuf[0:HALO, :] = pu_prev_ref[...] * not_first
    ubuf[HALO:HALO + ts, :] = pu_ref[...]
    ubuf[HALO + ts:, :] = pu_next_ref[...] * not_last

    for rc in range(ts // ROW_CHUNK):
        r0 = rc * ROW_CHUNK
        pos = i * ts + r0 + lax.broadcasted_iota(jnp.int32, (ROW_CHUNK, LANES), 0)
        mixed = []
        for gi, win in enumerate(POOL_WINDOWS):
            cols = slice(gi * C_G, (gi + 1) * C_G)
            half = win // 2
            tot = jnp.zeros((ROW_CHUNK, LANES), F32)
            for o in range(-half, win - half):
                tot = tot + ubuf[HALO + r0 + o:HALO + r0 + o + ROW_CHUNK, cols]
            lo = jnp.maximum(pos - half, 0)
            hi = jnp.minimum(pos + (win - half), seq_len)
            pooled = tot / (hi - lo).astype(F32) - ubuf[HALO + r0:HALO + r0 + ROW_CHUNK, cols]
            m = jnp.dot(pooled.astype(BF16), pw_ref[gi], preferred_element_type=F32)
            mixed.append(m * ps_ref[:, cols])
        ssq = sum(jnp.sum(m * m, axis=-1, keepdims=True) for m in mixed)
        scale = lax.rsqrt(ssq * (1.0 / C_C) + EPS)
        for gi, m in enumerate(mixed):
            cols = slice(gi * C_G, (gi + 1) * C_G)
            yc_ref[r0:r0 + ROW_CHUNK, cols] = (m * scale * gc_ref[:, cols]).astype(yc_ref.dtype)


def _local_mix(conv_u, pool_u, conv_w, conv_b, ln_g, ln_b, pool_w, pool_scale, g_a, g_c):
    S = conv_u.shape[0]
    ts = min(MIX_TS, S)
    hb = ts // HALO
    last_hb = S // HALO - 1
    prev_map = lambda i: (jnp.maximum(i * hb - 1, 0), 0)
    next_map = lambda i: (jnp.minimum((i + 1) * hb, last_hb), 0)
    row = lambda i: (i, 0)
    const2 = lambda i: (0, 0)
    return pl.pallas_call(
        functools.partial(_local_mix_kernel, seq_len=S),
        grid=(S // ts,),
        in_specs=[
            pl.BlockSpec((ts, 2 * C_A), row),
            pl.BlockSpec((HALO, 2 * C_A), prev_map),
            pl.BlockSpec((HALO, 2 * C_A), next_map),
            pl.BlockSpec((ts, C_C), row),
            pl.BlockSpec((HALO, C_C), prev_map),
            pl.BlockSpec((HALO, C_C), next_map),
            pl.BlockSpec((CONV_WIDTH, C_A), const2),
            pl.BlockSpec((1, C_A), const2),
            pl.BlockSpec((1, C_A), const2),
            pl.BlockSpec((1, C_A), const2),
            pl.BlockSpec((len(POOL_WINDOWS), C_G, C_G), lambda i: (0, 0, 0)),
            pl.BlockSpec((1, C_C), const2),
            pl.BlockSpec((1, C_A), const2),
            pl.BlockSpec((1, C_C), const2),
        ],
        out_specs=[pl.BlockSpec((ts, C_A), row), pl.BlockSpec((ts, C_C), row)],
        out_shape=[jax.ShapeDtypeStruct((S, C_A), BF16), jax.ShapeDtypeStruct((S, C_C), BF16)],
        scratch_shapes=[
            pltpu.VMEM((ts + 2 * HALO, C_A), F32),
            pltpu.VMEM((ts, C_A), F32),
            pltpu.VMEM((ts + 2 * HALO, C_C), F32),
        ],
        compiler_params=pltpu.CompilerParams(
            dimension_semantics=("arbitrary",), vmem_limit_bytes=VMEM_LIMIT),
        name="local_mix",
    )(conv_u, conv_u, conv_u, pool_u, pool_u, pool_u,
      conv_w, conv_b, ln_g, ln_b, pool_w, pool_scale, g_a, g_c)


ATT_TQ = 128
ATT_W = ATT_TQ + 2 * HALF_WINDOW
COPY_CHUNK = 256
ATT_UNROLL = 4


def _band_bias():
    r = np.arange(ATT_TQ)[:, None]
    c = np.arange(ATT_W)[None, :]
    band = (c - r >= 0) & (c - r <= 2 * HALF_WINDOW)
    first = band & (c >= HALF_WINDOW)
    last = band & (c < ATT_TQ + HALF_WINDOW)
    return np.where(np.stack([band, first, last]), 0.0, NEG).astype(np.float32)


def _attention_kernel(q_ref, k_ref, v_ref, bias_ref, o_ref, qd, kd, vd, m_acc, s_acc):
    S = o_ref.shape[0]
    lane = lax.broadcasted_iota(jnp.int32, (ATT_TQ, LANES), 1)
    head_a = lane < HEAD_DIM
    nt_dims = (((1,), (1,)), ((), ()))

    for pattern, d in enumerate(DILATIONS):
        L = S // d
        LP = L + 2 * HALF_WINDOW
        n_blk = L // ATT_TQ
        chunk = min(COPY_CHUNK, L)

        def copy_residue(r, carry):
            zpad = jnp.zeros((HALF_WINDOW, LANES), BF16)
            kbase = pl.multiple_of(r * LP, HALF_WINDOW)
            for buf in (kd, vd):
                buf[pl.ds(kbase, HALF_WINDOW), :] = zpad
                buf[pl.ds(kbase + HALF_WINDOW + L, HALF_WINDOW), :] = zpad

            def copy_chunk(c, carry2):
                src = pl.ds(r + d * c * chunk, chunk, stride=d)
                qdst = pl.multiple_of(r * L + c * chunk, chunk)
                kdst = pl.multiple_of(r * LP + HALF_WINDOW + c * chunk, HALF_WINDOW)
                qd[pl.ds(qdst, chunk), :] = q_ref[0, src, :].astype(BF16)
                kd[pl.ds(kdst, chunk), :] = k_ref[0, src, :].astype(BF16)
                vd[pl.ds(kdst, chunk), :] = v_ref[0, src, :].astype(BF16)
                return carry2

            return lax.fori_loop(0, L // chunk, copy_chunk, carry)

        lax.fori_loop(0, d, copy_residue, 0)

        def residue_blocks(r, carry):
            def block(jb, carry2):
                j0 = jb * ATT_TQ
                q2 = qd[pl.ds(pl.multiple_of(r * L + j0, ATT_TQ), ATT_TQ), :]
                kv0 = pl.multiple_of(r * LP + j0, HALF_WINDOW)
                kw = kd[pl.ds(kv0, ATT_W), :]
                vw = vd[pl.ds(kv0, ATT_W), :]
                sel = jnp.where(jb == 0, 1, jnp.where(jb == n_blk - 1, 2, 0))
                bias = bias_ref[sel]
                zero = jnp.zeros_like(q2)

                def one_head(qh):
                    s = lax.dot_general(qh, kw, nt_dims, preferred_element_type=F32) + bias
                    m = jnp.max(s, axis=-1, keepdims=True)
                    p = jnp.exp(s - m)
                    l = jnp.sum(p, axis=-1, keepdims=True)
                    o = jnp.dot(p.astype(BF16), vw, preferred_element_type=F32)
                    return m, l, o

                ma, la, oa = one_head(jnp.where(head_a, q2, zero))
                mb, lb, ob = one_head(jnp.where(head_a, zero, q2))
                m_blk = jnp.where(head_a, ma, mb)
                l_blk = jnp.where(head_a, la, lb)
                o_blk = jnp.where(head_a, oa, ob)

                rows = pl.ds(r + d * j0, ATT_TQ, stride=d)
                if pattern == 0:
                    m_acc[rows, :] = m_blk
                    s_acc[rows, :] = l_blk
                    o_ref[rows, :] = o_blk
                else:
                    m_old = m_acc[rows, :]
                    m_new = jnp.maximum(m_old, m_blk)
                    w_old = jnp.exp(m_old - m_new)
                    w_blk = jnp.exp(m_blk - m_new)
                    m_acc[rows, :] = m_new
                    s_acc[rows, :] = w_old * s_acc[rows, :] + w_blk * l_blk
                    o_ref[rows, :] = w_old * o_ref[rows, :] + w_blk * o_blk
                return carry2

            return lax.fori_loop(0, n_blk, block, carry, unroll=ATT_UNROLL)

        lax.fori_loop(0, d, residue_blocks, 0)

    def normalise(c, carry):
        rows = pl.ds(pl.multiple_of(c * COPY_CHUNK, COPY_CHUNK), COPY_CHUNK)
        o_ref[rows, :] = o_ref[rows, :] / s_acc[rows, :]
        return carry

    lax.fori_loop(0, S // COPY_CHUNK, normalise, 0)


def _attention(qkv, bias):
    S = qkv.shape[1]
    assert S % (DILATIONS[-1] * ATT_TQ * 2) == 0, "first and last key blocks must be distinct"
    pad_rows = DILATIONS[-1] * 2 * HALF_WINDOW
    qkv_spec = lambda off: pl.BlockSpec((1, S, LANES), lambda p: (p + off, 0, 0),
                                        pipeline_mode=pl.Buffered(1))
    return pl.pallas_call(
        _attention_kernel,
        grid=(N_PAIRS,),
        in_specs=[
            qkv_spec(0), qkv_spec(N_PAIRS), qkv_spec(2 * N_PAIRS),
            pl.BlockSpec((3, ATT_TQ, ATT_W), lambda p: (0, 0, 0)),
        ],
        out_specs=pl.BlockSpec((S, LANES), lambda p: (0, p)),
        out_shape=jax.ShapeDtypeStruct((S, C_B), F32),
        scratch_shapes=[
            pltpu.VMEM((S, LANES), BF16),
            pltpu.VMEM((S + pad_rows, LANES), BF16),
            pltpu.VMEM((S + pad_rows, LANES), BF16),
            pltpu.VMEM((S, LANES), F32),
            pltpu.VMEM((S, LANES), F32),
        ],
        compiler_params=pltpu.CompilerParams(
            dimension_semantics=("arbitrary",), vmem_limit_bytes=VMEM_LIMIT),
        name="attention",
    )(qkv, qkv, qkv, bias)


OUT_TM = 512


def _out_proj_kernel(ya_ref, yb_ref, yc_ref, gb_ref, w_ref, x_ref, gp_ref, o_ref):
    yb = yb_ref[...]
    ybn = (yb * _rms_scale(yb) * gb_ref[...]).astype(BF16)
    acc = jnp.dot(ya_ref[...], w_ref[0:C_A, :], preferred_element_type=F32)
    acc = acc + jnp.dot(ybn, w_ref[C_A:C_A + C_B, :], preferred_element_type=F32)
    acc = acc + jnp.dot(yc_ref[...], w_ref[C_A + C_B:, :], preferred_element_type=F32)
    o_ref[...] = x_ref[...] + acc * _rms_scale(acc) * gp_ref[...]


def _out_proj(ya, yb, yc, g_b, w, layer, x, g_post):
    S = x.shape[0]
    tm = min(OUT_TM, S)
    row = lambda i: (i, 0)
    const2 = lambda i: (0, 0)
    return pl.pallas_call(
        _out_proj_kernel,
        grid=(S // tm,),
        in_specs=[
            pl.BlockSpec((tm, C_A), row),
            pl.BlockSpec((tm, C_B), row),
            pl.BlockSpec((tm, C_C), row),
            pl.BlockSpec((1, C_B), const2),
            pl.BlockSpec((None, MIX_WIDTH, D_MODEL), lambda i: (layer, 0, 0)),
            pl.BlockSpec((tm, D_MODEL), row),
            pl.BlockSpec((1, D_MODEL), const2),
        ],
        out_specs=pl.BlockSpec((tm, D_MODEL), row),
        out_shape=jax.ShapeDtypeStruct((S, D_MODEL), F32),
        compiler_params=pltpu.CompilerParams(
            dimension_semantics=("arbitrary",), vmem_limit_bytes=VMEM_LIMIT),
        name="out_proj",
    )(ya, yb, yc, g_b, w, x, g_post)


FFN_TM = 512
FFN_TH = 512


def _ffn_kernel(x_ref, gpre_ref, wg_ref, wu_ref, wd_ref, gpost_ref, o_ref, h_ref, acc_ref):
    j = pl.program_id(1)

    @pl.when(j == 0)
    def _():
        x = x_ref[...]
        h_ref[...] = (x * _rms_scale(x) * gpre_ref[...]).astype(BF16)
        acc_ref[...] = jnp.zeros_like(acc_ref)

    h = h_ref[...]
    gate = jnp.dot(h, wg_ref[...], preferred_element_type=F32)
    up = jnp.dot(h, wu_ref[...], preferred_element_type=F32)
    act = (gate * jax.nn.sigmoid(gate) * up).astype(BF16)
    acc_ref[...] += jnp.dot(act, wd_ref[...], preferred_element_type=F32)

    @pl.when(j == pl.num_programs(1) - 1)
    def _():
        f = acc_ref[...]
        o_ref[...] = x_ref[...] + f * _rms_scale(f) * gpost_ref[...]


def _ffn(x, g_pre, w_gate, w_up, w_down, layer, g_post):
    S = x.shape[0]
    tm = min(FFN_TM, S)
    return pl.pallas_call(
        _ffn_kernel,
        grid=(S // tm, FFN_HIDDEN // FFN_TH),
        in_specs=[
            pl.BlockSpec((tm, D_MODEL), lambda i, j: (i, 0)),
            pl.BlockSpec((1, D_MODEL), lambda i, j: (0, 0)),
            pl.BlockSpec((None, D_MODEL, FFN_TH), lambda i, j: (layer, 0, j)),
            pl.BlockSpec((None, D_MODEL, FFN_TH), lambda i, j: (layer, 0, j)),
            pl.BlockSpec((None, FFN_TH, D_MODEL), lambda i, j: (layer, j, 0)),
            pl.BlockSpec((1, D_MODEL), lambda i, j: (0, 0)),
        ],
        out_specs=pl.BlockSpec((tm, D_MODEL), lambda i, j: (i, 0)),
        out_shape=jax.ShapeDtypeStruct((S, D_MODEL), F32),
        scratch_shapes=[pltpu.VMEM((tm, D_MODEL), BF16), pltpu.VMEM((tm, D_MODEL), F32)],
        compiler_params=pltpu.CompilerParams(
            dimension_semantics=("arbitrary", "arbitrary"), vmem_limit_bytes=VMEM_LIMIT),
        name="ffn",
    )(x, g_pre, w_gate, w_up, w_down, g_post)


def _rope_lane_tables(S):
    half = ROT_DIM // 2
    pos = jnp.arange(S, dtype=F32)
    inv = ROPE_THETA ** (-jnp.arange(0, ROT_DIM, 2, dtype=F32) / ROT_DIM)
    ang = pos[:, None] * inv[None, :]
    cos, sin = jnp.cos(ang), jnp.sin(ang)
    ones = jnp.ones((S, HEAD_DIM - ROT_DIM), F32)
    zeros_h = jnp.zeros((S, half), F32)
    zeros_r = jnp.zeros((S, HEAD_DIM - ROT_DIM), F32)
    cos_h = jnp.concatenate([cos, cos, ones], axis=1)
    sin_lo_h = jnp.concatenate([-sin, zeros_h, zeros_r], axis=1)
    sin_hi_h = jnp.concatenate([zeros_h, sin, zeros_r], axis=1)
    pair = lambda t: jnp.concatenate([t, t], axis=1)
    return pair(cos_h), pair(sin_lo_h), pair(sin_hi_h)


def kernel(x, w_in, conv_w, conv_b, conv_ln_g, conv_ln_b, pool_w, pool_scale, g_mix, w_out,
           g_pre_mix, g_post_mix, g_pre_ffn, g_post_ffn, w_gate, w_up, w_down):
    B, S, _ = x.shape
    depth = w_in.shape[0]
    cos_t, sin_lo_t, sin_hi_t = _rope_lane_tables(S)
    bias = jnp.asarray(_band_bias())
    row = lambda v: v.reshape(1, -1)
    w_in, w_out, w_gate, w_up, w_down, pool_w = (
        t.astype(BF16) for t in (w_in, w_out, w_gate, w_up, w_down, pool_w))

    assert B == 1, "the sequence-mixing kernels treat all rows as one sequence"
    xb = x.reshape(S, D_MODEL)
    for l in range(depth):
        conv_u, qkv, pool_u = _in_proj(xb, row(g_pre_mix[l]), w_in, l, cos_t, sin_lo_t, sin_hi_t)
        ya, yc = _local_mix(conv_u, pool_u, conv_w[l], row(conv_b[l]), row(conv_ln_g[l]),
                            row(conv_ln_b[l]), pool_w[l], row(pool_scale[l]),
                            row(g_mix[l, :C_A]), row(g_mix[l, C_A + C_B:]))
        yb = _attention(qkv, bias)
        xb = _out_proj(ya, yb, yc, row(g_mix[l, C_A:C_A + C_B]), w_out, l, xb, row(g_post_mix[l]))
        xb = _ffn(xb, row(g_pre_ffn[l]), w_gate, w_up, w_down, l, row(g_post_ffn[l]))
    return xb.reshape(B, S, D_MODEL)
```

```python
import functools

import numpy as np
import jax
import jax.numpy as jnp
from jax import lax
from jax.experimental import pallas as pl
from jax.experimental.pallas import tpu as pltpu

F32 = jnp.float32
BF16 = jnp.bfloat16

D_MODEL = 2048
C_A = 512
N_HEADS = 16
HEAD_DIM = 64
C_B = N_HEADS * HEAD_DIM
C_C = 512
POOL_WINDOWS = (2, 4, 8, 16)
C_G = C_C // len(POOL_WINDOWS)
MIX_WIDTH = C_A + C_B + C_C
IN_WIDTH = 2 * C_A + 3 * C_B + C_C
CONV_WIDTH = 31
CONV_HALF = CONV_WIDTH // 2
DILATIONS = (1, 4, 16)
HALF_WINDOW = 64
ROT_DIM = HEAD_DIM // 4
ROPE_THETA = 500000.0
FFN_HIDDEN = 5632
EPS = 1e-6
NEG = -1e30

LANES = 128
SUBLANES = 8
N_PAIRS = N_HEADS // 2
RUNS = DILATIONS[-1]
ROW_TILE = 512
VMEM_LIMIT = 52 * 1024 * 1024


def _rms_scale(t):
    return lax.rsqrt(jnp.mean(t * t, axis=-1, keepdims=True) + EPS)


def _row_tile(S):
    tm = min(ROW_TILE, S // RUNS)
    assert (S // RUNS) % tm == 0
    return tm


def _natural_rows(S, width):
    J = S // RUNS
    tm = _row_tile(S)
    nj = J // tm
    return (J, RUNS * width), pl.BlockSpec((tm, width), lambda i, *_: (i % nj, i // nj))


IN_TN = 512


def _in_proj_kernel(x_ref, g_ref, w_ref, cos_ref, sin_lo_ref, sin_hi_ref,
                    conv_ref, qkv_ref, pool_ref, h_ref):
    j = pl.program_id(1)

    @pl.when(j == 0)
    def _():
        x = x_ref[...]
        h_ref[...] = (x * _rms_scale(x) * g_ref[...]).astype(BF16)

    acc = jnp.dot(h_ref[...], w_ref[...], preferred_element_type=F32)
    n_slabs = IN_TN // LANES

    @pl.when(j < 2)
    def _():
        conv_ref[...] = acc

    @pl.when((j >= 2) & (j < 6))
    def _():
        scale = jnp.where(j < 4, HEAD_DIM ** -0.5, 1.0).astype(F32)
        for t in range(n_slabs):
            blk = acc[:, t * LANES:(t + 1) * LANES]
            rot = (blk * cos_ref[...]
                   + pltpu.roll(blk, ROT_DIM // 2, 1) * sin_hi_ref[...]
                   + pltpu.roll(blk, LANES - ROT_DIM // 2, 1) * sin_lo_ref[...])
            qkv_ref[t] = rot * scale

    @pl.when((j >= 6) & (j < 8))
    def _():
        for t in range(n_slabs):
            qkv_ref[t] = acc[:, t * LANES:(t + 1) * LANES]

    @pl.when(j == 8)
    def _():
        pool_ref[...] = acc


def _in_proj(x, x_is_natural, g, w, layer, cos_t, sin_lo_t, sin_hi_t):
    S = x.shape[0]
    tm = _row_tile(S)
    n_slabs = IN_TN // LANES
    if x_is_natural:
        view, x_spec = _natural_rows(S, D_MODEL)
        x = x.reshape(view)
    else:
        x_spec = pl.BlockSpec((tm, D_MODEL), lambda i, j: (i, 0))
    return pl.pallas_call(
        _in_proj_kernel,
        grid=(S // tm, IN_WIDTH // IN_TN),
        in_specs=[
            x_spec,
            pl.BlockSpec((1, D_MODEL), lambda i, j: (0, 0)),
            pl.BlockSpec((None, D_MODEL, IN_TN), lambda i, j: (layer, 0, j)),
            pl.BlockSpec((tm, LANES), lambda i, j: (i, 0)),
            pl.BlockSpec((tm, LANES), lambda i, j: (i, 0)),
            pl.BlockSpec((tm, LANES), lambda i, j: (i, 0)),
        ],
        out_specs=[
            pl.BlockSpec((tm, IN_TN), lambda i, j: (i, jnp.minimum(j, 1))),
            pl.BlockSpec((n_slabs, tm, LANES), lambda i, j: (jnp.clip(j - 2, 0, 5), i, 0)),
            pl.BlockSpec((tm, IN_TN), lambda i, j: (i, 0)),
        ],
        out_shape=[
            jax.ShapeDtypeStruct((S, 2 * C_A), F32),
            jax.ShapeDtypeStruct((3 * N_PAIRS, S, LANES), F32),
            jax.ShapeDtypeStruct((S, C_C), F32),
        ],
        scratch_shapes=[pltpu.VMEM((tm, D_MODEL), BF16)],
        compiler_params=pltpu.CompilerParams(
            dimension_semantics=("arbitrary", "arbitrary"), vmem_limit_bytes=VMEM_LIMIT),
        name="in_proj",
    )(x, g, w, cos_t, sin_lo_t, sin_hi_t)


MIX_TJ = 32
HALO = SUBLANES


def _tap_source(r, offset):
    q = r + offset
    return q % RUNS, q // RUNS


def _local_mix_kernel(cu_ref, cu_prev_ref, cu_next_ref, pu_ref, pu_prev_ref, pu_next_ref,
                      cw_ref, cb_ref, lng_ref, lnb_ref, pw_ref, ps_ref, ga_ref, gc_ref,
                      ya_ref, yc_ref, hbuf, h_dn, h_up, cbuf, ubuf, u_dn, u_up, pbuf, *, seq_len):
    i = pl.program_id(0)
    n = pl.num_programs(0)
    tj = ya_ref.shape[1]
    not_first = (i > 0).astype(F32)
    not_last = (i < n - 1).astype(F32)
    mid = slice(HALO, HALO + tj)

    def glu(u):
        return u[:, :C_A] * jax.nn.sigmoid(u[:, C_A:])

    for r in range(RUNS):
        hbuf[r, 0:HALO, :] = glu(cu_prev_ref[r]) * not_first
        hbuf[r, mid, :] = glu(cu_ref[r])
        hbuf[r, HALO + tj:, :] = glu(cu_next_ref[r]) * not_last
        ubuf[r, 0:HALO, :] = pu_prev_ref[r] * not_first
        ubuf[r, mid, :] = pu_ref[r]
        ubuf[r, HALO + tj:, :] = pu_next_ref[r] * not_last
    for r in range(RUNS):
        h_dn[r] = hbuf[r, HALO - 1:HALO - 1 + tj, :]
        h_up[r] = hbuf[r, HALO + 1:HALO + 1 + tj, :]
        u_dn[r] = ubuf[r, HALO - 1:HALO - 1 + tj, :]
        u_up[r] = ubuf[r, HALO + 1:HALO + 1 + tj, :]

    def shifted(center, dn, up, run, shift, cols):
        if shift == 0:
            return center[run, mid, cols]
        return (dn if shift < 0 else up)[run, :, cols]

    for r in range(RUNS):
        for lc in range(C_A // LANES):
            cols = slice(lc * LANES, (lc + 1) * LANES)
            acc = jnp.zeros((tj, LANES), F32)
            for t in range(CONV_WIDTH):
                run, shift = _tap_source(r, t - CONV_HALF)
                acc = acc + shifted(hbuf, h_dn, h_up, run, shift, cols) * cw_ref[t:t + 1, cols]
            cbuf[r, :, cols] = acc + cb_ref[:, cols]

    for r in range(RUNS):
        c = cbuf[r]
        mu = jnp.mean(c, axis=-1, keepdims=True)
        cc = c - mu
        var = jnp.mean(cc * cc, axis=-1, keepdims=True)
        y = cc * lax.rsqrt(var + EPS) * lng_ref[...] + lnb_ref[...]
        y = y * jax.nn.sigmoid(y)
        ya_ref[r] = (y * _rms_scale(y) * ga_ref[...]).astype(ya_ref.dtype)

    row = lax.broadcasted_iota(jnp.int32, (tj, LANES), 0)
    for r in range(RUNS):
        pos = (i * tj + row) * RUNS + r
        for gi, win in enumerate(POOL_WINDOWS):
            cols = slice(gi * C_G, (gi + 1) * C_G)
            half = win // 2
            tot = jnp.zeros((tj, LANES), F32)
            for o in range(-half, win - half):
                run, shift = _tap_source(r, o)
                tot = tot + shifted(ubuf, u_dn, u_up, run, shift, cols)
            lo = jnp.maximum(pos - half, 0)
            hi = jnp.minimum(pos + (win - half), seq_len)
            pbuf[r * tj:(r + 1) * tj, cols] = tot / (hi - lo).astype(F32) - ubuf[r, mid, cols]

    mixed = []
    for gi in range(len(POOL_WINDOWS)):
        cols = slice(gi * C_G, (gi + 1) * C_G)
        m = jnp.dot(pbuf[:, cols].astype(BF16), pw_ref[gi], preferred_element_type=F32)
        mixed.append(m * ps_ref[:, cols])
    ssq = sum(jnp.sum(m * m, axis=-1, keepdims=True) for m in mixed)
    scale = lax.rsqrt(ssq * (1.0 / C_C) + EPS)
    for gi, m in enumerate(mixed):
        cols = slice(gi * C_G, (gi + 1) * C_G)
        y = (m * scale * gc_ref[:, cols]).astype(yc_ref.dtype)
        for r in range(RUNS):
            yc_ref[r, :, cols] = y[r * tj:(r + 1) * tj]


def _local_mix(conv_u, pool_u, conv_w, conv_b, ln_g, ln_b, pool_w, pool_scale, g_a, g_c):
    S = conv_u.shape[0]
    J = S // RUNS
    tj = min(MIX_TJ, J)
    hb = tj // HALO
    last_hb = J // HALO - 1
    cur_map = lambda i: (0, i, 0)
    prev_map = lambda i: (0, jnp.maximum(i * hb - 1, 0), 0)
    next_map = lambda i: (0, jnp.minimum((i + 1) * hb, last_hb), 0)
    const2 = lambda i: (0, 0)
    conv_u = conv_u.reshape(RUNS, J, 2 * C_A)
    pool_u = pool_u.reshape(RUNS, J, C_C)
    ya, yc = pl.pallas_call(
        functools.partial(_local_mix_kernel, seq_len=S),
        grid=(J // tj,),
        in_specs=[
            pl.BlockSpec((RUNS, tj, 2 * C_A), cur_map),
            pl.BlockSpec((RUNS, HALO, 2 * C_A), prev_map),
            pl.BlockSpec((RUNS, HALO, 2 * C_A), next_map),
            pl.BlockSpec((RUNS, tj, C_C), cur_map),
            pl.BlockSpec((RUNS, HALO, C_C), prev_map),
            pl.BlockSpec((RUNS, HALO, C_C), next_map),
            pl.BlockSpec((CONV_WIDTH, C_A), const2),
            pl.BlockSpec((1, C_A), const2),
            pl.BlockSpec((1, C_A), const2),
            pl.BlockSpec((1, C_A), const2),
            pl.BlockSpec((len(POOL_WINDOWS), C_G, C_G), lambda i: (0, 0, 0)),
            pl.BlockSpec((1, C_C), const2),
            pl.BlockSpec((1, C_A), const2),
            pl.BlockSpec((1, C_C), const2),
        ],
        out_specs=[pl.BlockSpec((RUNS, tj, C_A), cur_map), pl.BlockSpec((RUNS, tj, C_C), cur_map)],
        out_shape=[jax.ShapeDtypeStruct((RUNS, J, C_A), BF16),
                   jax.ShapeDtypeStruct((RUNS, J, C_C), BF16)],
        scratch_shapes=[
            pltpu.VMEM((RUNS, tj + 2 * HALO, C_A), F32),
            pltpu.VMEM((RUNS, tj, C_A), F32),
            pltpu.VMEM((RUNS, tj, C_A), F32),
            pltpu.VMEM((RUNS, tj, C_A), F32),
            pltpu.VMEM((RUNS, tj + 2 * HALO, C_C), F32),
            pltpu.VMEM((RUNS, tj, C_C), F32),
            pltpu.VMEM((RUNS, tj, C_C), F32),
            pltpu.VMEM((RUNS * tj, C_C), F32),
        ],
        compiler_params=pltpu.CompilerParams(
            dimension_semantics=("arbitrary",), vmem_limit_bytes=VMEM_LIMIT),
        name="local_mix",
    )(conv_u, conv_u, conv_u, pool_u, pool_u, pool_u,
      conv_w, conv_b, ln_g, ln_b, pool_w, pool_scale, g_a, g_c)
    return ya.reshape(S, C_A), yc.reshape(S, C_C)


ATT_TQ = 128
ATT_W = ATT_TQ + 2 * HALF_WINDOW
ATT_UNROLL = 8
INTERIOR, FIRST, LAST = 0, 1, 2


def _block_layout(d, kind):
    pieces = RUNS // d
    q_rows, k_rows = ATT_TQ // pieces, ATT_W // pieces
    first_group = {INTERIOR: -HALF_WINDOW // pieces, FIRST: 0, LAST: (ATT_TQ - ATT_W) // pieces}[kind]
    a_q, u_q = np.divmod(np.arange(ATT_TQ), q_rows)
    a_k, u_k = np.divmod(np.arange(ATT_W), k_rows)
    return pieces * u_q + a_q, pieces * (u_k + first_group) + a_k, first_group


def _band_bias():
    tables = []
    for d in DILATIONS:
        for kind in (INTERIOR, FIRST, LAST):
            q_step, k_step, _ = _block_layout(d, kind)
            ok = np.abs(k_step[None, :] - q_step[:, None]) <= HALF_WINDOW
            tables.append(np.where(ok, 0.0, NEG))
    return np.stack(tables).astype(np.float32)


def _attention_kernel(q_ref, k_ref, v_ref, bias_ref, o_ref, m_acc, s_acc):
    S = o_ref.shape[0]
    J = S // RUNS
    lane = lax.broadcasted_iota(jnp.int32, (ATT_TQ, LANES), 1)
    head_a = lane < HEAD_DIM
    nt_dims = (((1,), (1,)), ((), ()))

    def gather(ref, starts, rows):
        return jnp.concatenate([ref[pl.ds(s, rows), :] for s in starts], axis=0)

    def attend(q, kw, vw, bias):
        q2, kw, vw = q.astype(BF16), kw.astype(BF16), vw.astype(BF16)
        zero = jnp.zeros_like(q2)

        def one_head(qh):
            s = lax.dot_general(qh, kw, nt_dims, preferred_element_type=F32) + bias
            m = jnp.max(s, axis=-1, keepdims=True)
            p = jnp.exp(s - m)
            l = jnp.sum(p, axis=-1, keepdims=True)
            o = jnp.dot(p.astype(BF16), vw, preferred_element_type=F32)
            return m, l, o

        ma, la, oa = one_head(jnp.where(head_a, q2, zero))
        mb, lb, ob = one_head(jnp.where(head_a, zero, q2))
        return jnp.where(head_a, ma, mb), jnp.where(head_a, la, lb), jnp.where(head_a, oa, ob)

    def block(pattern, kind, q_starts, k_starts, bias):
        pieces = len(q_starts)
        q_rows, k_rows = ATT_TQ // pieces, ATT_W // pieces
        q = gather(q_ref.at[0], q_starts, q_rows)
        kw = gather(k_ref.at[0], k_starts, k_rows)
        vw = gather(v_ref.at[0], k_starts, k_rows)
        m_blk, l_blk, o_blk = attend(q, kw, vw, bias)
        if pattern > 0:
            m_old = gather(m_acc, q_starts, q_rows)
            m_new = jnp.maximum(m_old, m_blk)
            w_old = jnp.exp(m_old - m_new)
            w_blk = jnp.exp(m_blk - m_new)
            l_blk = w_old * gather(s_acc, q_starts, q_rows) + w_blk * l_blk
            o_blk = w_old * gather(o_ref, q_starts, q_rows) + w_blk * o_blk
            m_blk = m_new
        for a, s in enumerate(q_starts):
            rows = slice(a * q_rows, (a + 1) * q_rows)
            m_acc[pl.ds(s, q_rows), :] = m_blk[rows]
            s_acc[pl.ds(s, q_rows), :] = l_blk[rows]
            o_ref[pl.ds(s, q_rows), :] = o_blk[rows]

    def aligned(x):
        if isinstance(x, int):
            assert x % SUBLANES == 0
            return x
        return pl.multiple_of(x, SUBLANES)

    for pattern, d in enumerate(DILATIONS):
        pieces = RUNS // d
        q_rows, k_rows = ATT_TQ // pieces, ATT_W // pieces
        n_blk = (S // d) // ATT_TQ
        assert n_blk >= 2, "first and last key blocks must be distinct"
        first_groups = [_block_layout(d, kind)[2] for kind in (INTERIOR, FIRST, LAST)]

        def run_block(sub, jb, kind, pattern=pattern, d=d, pieces=pieces, q_rows=q_rows,
                      first_groups=first_groups):
            g0 = jb * q_rows
            q_starts = [aligned((sub + d * a) * J + g0) for a in range(pieces)]
            if kind == INTERIOR and first_groups[kind] % SUBLANES:
                lead = SUBLANES - (-first_groups[kind]) % SUBLANES
                k_starts = [aligned((sub + d * a) * J + g0 - lead + first_groups[kind]) + lead
                            for a in range(pieces)]
            else:
                k_starts = [aligned((sub + d * a) * J + g0 + first_groups[kind]) for a in range(pieces)]
            block(pattern, kind, q_starts, k_starts, bias_ref[3 * pattern + kind])

        def kind_of(jb, n_blk=n_blk):
            return FIRST if jb == 0 else LAST if jb == n_blk - 1 else INTERIOR

        group = ATT_UNROLL
        if n_blk <= group:
            subs_per_trip = group // n_blk

            def short_subs(t, carry, run_block=run_block, n_blk=n_blk, subs_per_trip=subs_per_trip,
                           kind_of=kind_of):
                for k in range(subs_per_trip):
                    for jb in range(n_blk):
                        run_block(t * subs_per_trip + k, jb, kind_of(jb))
                return carry

            assert d % subs_per_trip == 0
            lax.fori_loop(0, d // subs_per_trip, short_subs, 0)
        else:
            n_mid = n_blk - 2 * group
            assert n_mid >= 0 and n_mid % group == 0

            def long_sub(sub, carry, run_block=run_block, n_blk=n_blk, n_mid=n_mid, kind_of=kind_of,
                         group=group):
                for jb in range(group):
                    run_block(sub, jb, kind_of(jb))

                def middle(t, c):
                    for k in range(group):
                        run_block(sub, group + t * group + k, INTERIOR)
                    return c

                if n_mid:
                    lax.fori_loop(0, n_mid // group, middle, 0)
                for jb in range(n_blk - group, n_blk):
                    run_block(sub, jb, kind_of(jb))
                return carry

            if d == 1:
                long_sub(0, 0)
            else:
                lax.fori_loop(0, d, long_sub, 0)

    def normalise(c, carry):
        rows = pl.ds(pl.multiple_of(c * ATT_W, ATT_W), ATT_W)
        o_ref[rows, :] = o_ref[rows, :] / s_acc[rows, :]
        return carry

    lax.fori_loop(0, S // ATT_W, normalise, 0)


def _attention(qkv, bias):
    S = qkv.shape[1]
    qkv_spec = lambda off: pl.BlockSpec((1, S, LANES), lambda p: (p + off, 0, 0))
    return pl.pallas_call(
        _attention_kernel,
        grid=(N_PAIRS,),
        in_specs=[
            qkv_spec(0), qkv_spec(N_PAIRS), qkv_spec(2 * N_PAIRS),
            pl.BlockSpec(bias.shape, lambda p: (0, 0, 0)),
        ],
        out_specs=pl.BlockSpec((S, LANES), lambda p: (0, p)),
        out_shape=jax.ShapeDtypeStruct((S, C_B), F32),
        scratch_shapes=[pltpu.VMEM((S, LANES), F32), pltpu.VMEM((S, LANES), F32)],
        compiler_params=pltpu.CompilerParams(
            dimension_semantics=("arbitrary",), vmem_limit_bytes=VMEM_LIMIT),
        name="attention",
    )(qkv, qkv, qkv, bias)


def _out_proj_kernel(ya_ref, yb_ref, yc_ref, gb_ref, w_ref, x_ref, gp_ref, o_ref):
    yb = yb_ref[...]
    ybn = (yb * _rms_scale(yb) * gb_ref[...]).astype(BF16)
    acc = jnp.dot(ya_ref[...], w_ref[0:C_A, :], preferred_element_type=F32)
    acc = acc + jnp.dot(ybn, w_ref[C_A:C_A + C_B, :], preferred_element_type=F32)
    acc = acc + jnp.dot(yc_ref[...], w_ref[C_A + C_B:, :], preferred_element_type=F32)
    o_ref[...] = x_ref[...] + acc * _rms_scale(acc) * gp_ref[...]


def _out_proj(ya, yb, yc, g_b, w, layer, x, x_is_natural, g_post):
    S = ya.shape[0]
    tm = _row_tile(S)
    row = lambda i: (i, 0)
    const2 = lambda i: (0, 0)
    if x_is_natural:
        view, x_spec = _natural_rows(S, D_MODEL)
        x = x.reshape(view)
    else:
        x_spec = pl.BlockSpec((tm, D_MODEL), row)
    return pl.pallas_call(
        _out_proj_kernel,
        grid=(S // tm,),
        in_specs=[
            pl.BlockSpec((tm, C_A), row),
            pl.BlockSpec((tm, C_B), row),
            pl.BlockSpec((tm, C_C), row),
            pl.BlockSpec((1, C_B), const2),
            pl.BlockSpec((None, MIX_WIDTH, D_MODEL), lambda i: (layer, 0, 0)),
            x_spec,
            pl.BlockSpec((1, D_MODEL), const2),
        ],
        out_specs=pl.BlockSpec((tm, D_MODEL), row),
        out_shape=jax.ShapeDtypeStruct((S, D_MODEL), F32),
        compiler_params=pltpu.CompilerParams(
            dimension_semantics=("arbitrary",), vmem_limit_bytes=VMEM_LIMIT),
        name="out_proj",
    )(ya, yb, yc, g_b, w, x, g_post)


FFN_TH = 512


def _ffn_kernel(x_ref, gpre_ref, wg_ref, wu_ref, wd_ref, gpost_ref, o_ref, h_ref, acc_ref):
    j = pl.program_id(1)

    @pl.when(j == 0)
    def _():
        x = x_ref[...]
        h_ref[...] = (x * _rms_scale(x) * gpre_ref[...]).astype(BF16)
        acc_ref[...] = jnp.zeros_like(acc_ref)

    h = h_ref[...]
    gate = jnp.dot(h, wg_ref[...], preferred_element_type=F32)
    up = jnp.dot(h, wu_ref[...], preferred_element_type=F32)
    act = (gate * jax.nn.sigmoid(gate) * up).astype(BF16)
    acc_ref[...] += jnp.dot(act, wd_ref[...], preferred_element_type=F32)

    @pl.when(j == pl.num_programs(1) - 1)
    def _():
        f = acc_ref[...]
        o_ref[...] = x_ref[...] + f * _rms_scale(f) * gpost_ref[...]


def _ffn(x, g_pre, w_gate, w_up, w_down, layer, g_post, out_is_natural):
    S = x.shape[0]
    tm = _row_tile(S)
    if out_is_natural:
        view, out_spec = _natural_rows(S, D_MODEL)
        out_shape = jax.ShapeDtypeStruct(view, F32)
    else:
        out_spec = pl.BlockSpec((tm, D_MODEL), lambda i, j: (i, 0))
        out_shape = jax.ShapeDtypeStruct((S, D_MODEL), F32)
    out = pl.pallas_call(
        _ffn_kernel,
        grid=(S // tm, FFN_HIDDEN // FFN_TH),
        in_specs=[
            pl.BlockSpec((tm, D_MODEL), lambda i, j: (i, 0)),
            pl.BlockSpec((1, D_MODEL), lambda i, j: (0, 0)),
            pl.BlockSpec((None, D_MODEL, FFN_TH), lambda i, j: (layer, 0, j)),
            pl.BlockSpec((None, D_MODEL, FFN_TH), lambda i, j: (layer, 0, j)),
            pl.BlockSpec((None, FFN_TH, D_MODEL), lambda i, j: (layer, j, 0)),
            pl.BlockSpec((1, D_MODEL), lambda i, j: (0, 0)),
        ],
        out_specs=out_spec,
        out_shape=out_shape,
        scratch_shapes=[pltpu.VMEM((tm, D_MODEL), BF16), pltpu.VMEM((tm, D_MODEL), F32)],
        compiler_params=pltpu.CompilerParams(
            dimension_semantics=("arbitrary", "arbitrary"), vmem_limit_bytes=VMEM_LIMIT),
        name="ffn",
    )(x, g_pre, w_gate, w_up, w_down, g_post)
    return out.reshape(S, D_MODEL)


def _rope_lane_tables(S):
    half = ROT_DIM // 2
    J = S // RUNS
    pos = (jnp.arange(J, dtype=F32)[None, :] * RUNS + jnp.arange(RUNS, dtype=F32)[:, None]).reshape(S)
    inv = ROPE_THETA ** (-jnp.arange(0, ROT_DIM, 2, dtype=F32) / ROT_DIM)
    ang = pos[:, None] * inv[None, :]
    cos, sin = jnp.cos(ang), jnp.sin(ang)
    ones = jnp.ones((S, HEAD_DIM - ROT_DIM), F32)
    zeros_h = jnp.zeros((S, half), F32)
    zeros_r = jnp.zeros((S, HEAD_DIM - ROT_DIM), F32)
    cos_h = jnp.concatenate([cos, cos, ones], axis=1)
    sin_lo_h = jnp.concatenate([-sin, zeros_h, zeros_r], axis=1)
    sin_hi_h = jnp.concatenate([zeros_h, sin, zeros_r], axis=1)
    pair = lambda t: jnp.concatenate([t, t], axis=1)
    return pair(cos_h), pair(sin_lo_h), pair(sin_hi_h)


def kernel(x, w_in, conv_w, conv_b, conv_ln_g, conv_ln_b, pool_w, pool_scale, g_mix, w_out,
           g_pre_mix, g_post_mix, g_pre_ffn, g_post_ffn, w_gate, w_up, w_down):
    B, S, _ = x.shape
    depth = w_in.shape[0]
    assert B == 1, "the sequence-mixing kernels treat all rows as one sequence"
    cos_t, sin_lo_t, sin_hi_t = _rope_lane_tables(S)
    bias = jnp.asarray(_band_bias())
    row = lambda v: v.reshape(1, -1)
    w_in, w_out, w_gate, w_up, w_down, pool_w = (
        t.astype(BF16) for t in (w_in, w_out, w_gate, w_up, w_down, pool_w))

    xb = x.reshape(S, D_MODEL)
    for l in range(depth):
        natural = l == 0
        conv_u, qkv, pool_u = _in_proj(xb, natural, row(g_pre_mix[l]), w_in, l, cos_t, sin_lo_t, sin_hi_t)
        ya, yc = _local_mix(conv_u, pool_u, conv_w[l], row(conv_b[l]), row(conv_ln_g[l]),
                            row(conv_ln_b[l]), pool_w[l], row(pool_scale[l]),
                            row(g_mix[l, :C_A]), row(g_mix[l, C_A + C_B:]))
        yb = _attention(qkv, bias)
        xb = _out_proj(ya, yb, yc, row(g_mix[l, C_A:C_A + C_B]), w_out, l, xb, natural, row(g_post_mix[l]))
        xb = _ffn(xb, row(g_pre_ffn[l]), w_gate, w_up, w_down, l, row(g_post_ffn[l]),
                  out_is_natural=l == depth - 1)
    return xb.reshape(B, S, D_MODEL)
```

```python
import functools

import numpy as np
import jax
import jax.numpy as jnp
from jax import lax
from jax.experimental import pallas as pl
from jax.experimental.pallas import tpu as pltpu

F32 = jnp.float32
BF16 = jnp.bfloat16

D_MODEL = 2048
C_A = 512
N_HEADS = 16
HEAD_DIM = 64
C_B = N_HEADS * HEAD_DIM
C_C = 512
POOL_WINDOWS = (2, 4, 8, 16)
C_G = C_C // len(POOL_WINDOWS)
MIX_WIDTH = C_A + C_B + C_C
IN_WIDTH = 2 * C_A + 3 * C_B + C_C
CONV_WIDTH = 31
CONV_HALF = CONV_WIDTH // 2
DILATIONS = (1, 4, 16)
HALF_WINDOW = 64
ROT_DIM = HEAD_DIM // 4
ROPE_THETA = 500000.0
FFN_HIDDEN = 5632
EPS = 1e-6
NEG = -1e30

LANES = 128
SUBLANES = 8
N_PAIRS = N_HEADS // 2
RUNS = DILATIONS[-1]
ROW_TILE = 512
VMEM_LIMIT = 52 * 1024 * 1024


def _rms_scale(t):
    return lax.rsqrt(jnp.mean(t * t, axis=-1, keepdims=True) + EPS)


def _row_tile(S):
    tm = min(ROW_TILE, S // RUNS)
    assert (S // RUNS) % tm == 0 and tm % (RUNS * 2 * SUBLANES) == 0
    return tm


def _rows_spec(S, width, mixed):
    J, tm = S // RUNS, _row_tile(S)
    if mixed:
        return (RUNS, J, width), pl.BlockSpec((RUNS, tm // RUNS, width), lambda i, *_: (0, i, 0))
    return (S, width), pl.BlockSpec((tm, width), lambda i, *_: (i, 0))


def _natural_spec(S, width):
    J, tm = S // RUNS, _row_tile(S)
    return (J, RUNS, width), pl.BlockSpec((tm // RUNS, RUNS, width), lambda i, *_: (i, 0, 0))


def _load_rows(ref):
    v = ref[...]
    return v.reshape(-1, v.shape[-1]) if v.ndim == 3 else v


def _regroup_pitch(rows):
    return rows + SUBLANES if (rows // SUBLANES) % 2 == 0 else rows


def _regroup_by_run(x_ref, by_run_ref):
    for j in range(x_ref.shape[0]):
        by_run_ref[:, j, :] = x_ref[j]


IN_TN = 1536
IN_CHUNK = 256
N_SLABS = IN_WIDTH // LANES
Q_SLAB, K_SLAB, V_SLAB, POOL_SLAB = 8, 16, 24, 32


def _in_proj_kernel(x_ref, g_ref, w_ref, cos_ref, sin_lo_ref, sin_hi_ref, proj_ref, h_ref,
                    *by_run, x_is_natural):
    j = pl.program_id(1)
    tm = h_ref.shape[0]

    @pl.when(j == 0)
    def _():
        if x_is_natural:
            rows = tm // RUNS
            _regroup_by_run(x_ref, by_run[0])
            for r in range(RUNS):
                x = by_run[0][r, 0:rows, :]
                h_ref[r * rows:(r + 1) * rows, :] = (x * _rms_scale(x) * g_ref[...]).astype(BF16)
        else:
            x = x_ref[...]
            h_ref[...] = (x * _rms_scale(x) * g_ref[...]).astype(BF16)

    slabs_per_step = IN_TN // LANES
    for step in range(IN_WIDTH // IN_TN):
        @pl.when(j == step)
        def _(step=step):
            for c in range(IN_TN // IN_CHUNK):
                acc = jnp.dot(h_ref[...], w_ref[:, c * IN_CHUNK:(c + 1) * IN_CHUNK],
                              preferred_element_type=F32)
                for t in range(IN_CHUNK // LANES):
                    local = c * (IN_CHUNK // LANES) + t
                    slab = step * slabs_per_step + local
                    blk = acc[:, t * LANES:(t + 1) * LANES]
                    if Q_SLAB <= slab < V_SLAB:
                        blk = (blk * _load_rows(cos_ref)
                               + pltpu.roll(blk, ROT_DIM // 2, 1) * _load_rows(sin_hi_ref)
                               + pltpu.roll(blk, LANES - ROT_DIM // 2, 1) * _load_rows(sin_lo_ref))
                        if slab < K_SLAB:
                            blk = blk * HEAD_DIM ** -0.5
                    proj_ref[local] = blk.reshape(proj_ref.shape[1:])


def _in_proj(x, x_is_natural, g, w, layer, cos_t, sin_lo_t, sin_hi_t):
    S = x.shape[0]
    J, tm = S // RUNS, _row_tile(S)
    nj = J // tm
    slabs_per_step = IN_TN // LANES
    if x_is_natural:
        view, x_spec = _natural_spec(S, D_MODEL)
        tab_view, tab_spec = _rows_spec(S, LANES, mixed=True)
        out_spec = pl.BlockSpec((slabs_per_step, RUNS, tm // RUNS, LANES), lambda i, j: (j, 0, i, 0))
    else:
        view, x_spec = _rows_spec(S, D_MODEL, mixed=False)
        tab_view, tab_spec = _rows_spec(S, LANES, mixed=False)
        out_spec = pl.BlockSpec((slabs_per_step, None, tm, LANES), lambda i, j: (j, i // nj, i % nj, 0))
    tables = [t.reshape(tab_view) for t in (cos_t, sin_lo_t, sin_hi_t)]
    return pl.pallas_call(
        functools.partial(_in_proj_kernel, x_is_natural=x_is_natural),
        grid=(S // tm, IN_WIDTH // IN_TN),
        in_specs=[
            x_spec,
            pl.BlockSpec((1, D_MODEL), lambda i, j: (0, 0)),
            pl.BlockSpec((None, D_MODEL, IN_TN), lambda i, j: (layer, 0, j)),
            tab_spec, tab_spec, tab_spec,
        ],
        out_specs=out_spec,
        out_shape=jax.ShapeDtypeStruct((N_SLABS, RUNS, J, LANES), F32),
        scratch_shapes=[pltpu.VMEM((tm, D_MODEL), BF16)] + (
            [pltpu.VMEM((RUNS, _regroup_pitch(tm // RUNS), D_MODEL), F32)] if x_is_natural else []),
        compiler_params=pltpu.CompilerParams(
            dimension_semantics=("arbitrary", "arbitrary"), vmem_limit_bytes=VMEM_LIMIT),
        name="in_proj",
    )(x.reshape(view), g, w, *tables)


MIX_TJ = 32
HALO = SUBLANES


def _tap_source(r, offset):
    q = r + offset
    return q % RUNS, q // RUNS


def _local_mix_kernel(cu_ref, cu_prev_ref, cu_next_ref, pu_ref, pu_prev_ref, pu_next_ref,
                      cw_ref, cb_ref, lng_ref, lnb_ref, pw_ref, ps_ref, ga_ref, gc_ref,
                      ya_ref, yc_ref, hbuf, h_dn, h_up, cbuf, ubuf, u_dn, u_up, pbuf, *, seq_len):
    i = pl.program_id(0)
    n = pl.num_programs(0)
    tj = ya_ref.shape[1]
    not_first = (i > 0).astype(F32)
    not_last = (i < n - 1).astype(F32)
    mid = slice(HALO, HALO + tj)

    def glu(ref, lc, r):
        return ref[lc, r] * jax.nn.sigmoid(ref[C_A // LANES + lc, r])

    n_lc = C_A // LANES
    for r in range(RUNS):
        for lc in range(n_lc):
            cols = slice(lc * LANES, (lc + 1) * LANES)
            hbuf[r, 0:HALO, cols] = glu(cu_prev_ref, lc, r) * not_first
            hbuf[r, mid, cols] = glu(cu_ref, lc, r)
            hbuf[r, HALO + tj:, cols] = glu(cu_next_ref, lc, r) * not_last
        for gi in range(len(POOL_WINDOWS)):
            cols = slice(gi * C_G, (gi + 1) * C_G)
            ubuf[r, 0:HALO, cols] = pu_prev_ref[gi, r] * not_first
            ubuf[r, mid, cols] = pu_ref[gi, r]
            ubuf[r, HALO + tj:, cols] = pu_next_ref[gi, r] * not_last
    for r in range(RUNS):
        h_dn[r] = hbuf[r, HALO - 1:HALO - 1 + tj, :]
        h_up[r] = hbuf[r, HALO + 1:HALO + 1 + tj, :]
        u_dn[r] = ubuf[r, HALO - 1:HALO - 1 + tj, :]
        u_up[r] = ubuf[r, HALO + 1:HALO + 1 + tj, :]

    def shifted(center, dn, up, run, shift, cols):
        if shift == 0:
            return center[run, mid, cols]
        return (dn if shift < 0 else up)[run, :, cols]

    for r in range(RUNS):
        for lc in range(C_A // LANES):
            cols = slice(lc * LANES, (lc + 1) * LANES)
            acc = jnp.zeros((tj, LANES), F32)
            for t in range(CONV_WIDTH):
                run, shift = _tap_source(r, t - CONV_HALF)
                acc = acc + shifted(hbuf, h_dn, h_up, run, shift, cols) * cw_ref[t:t + 1, cols]
            cbuf[r, :, cols] = acc + cb_ref[:, cols]

    for r in range(RUNS):
        c = cbuf[r]
        mu = jnp.mean(c, axis=-1, keepdims=True)
        cc = c - mu
        var = jnp.mean(cc * cc, axis=-1, keepdims=True)
        y = cc * lax.rsqrt(var + EPS) * lng_ref[...] + lnb_ref[...]
        y = y * jax.nn.sigmoid(y)
        ya_ref[r] = (y * _rms_scale(y) * ga_ref[...]).astype(ya_ref.dtype)

    row = lax.broadcasted_iota(jnp.int32, (tj, LANES), 0)
    for r in range(RUNS):
        pos = (i * tj + row) * RUNS + r
        for gi, win in enumerate(POOL_WINDOWS):
            cols = slice(gi * C_G, (gi + 1) * C_G)
            half = win // 2
            tot = jnp.zeros((tj, LANES), F32)
            for o in range(-half, win - half):
                run, shift = _tap_source(r, o)
                tot = tot + shifted(ubuf, u_dn, u_up, run, shift, cols)
            lo = jnp.maximum(pos - half, 0)
            hi = jnp.minimum(pos + (win - half), seq_len)
            pbuf[r * tj:(r + 1) * tj, cols] = tot / (hi - lo).astype(F32) - ubuf[r, mid, cols]

    mixed = []
    for gi in range(len(POOL_WINDOWS)):
        cols = slice(gi * C_G, (gi + 1) * C_G)
        m = jnp.dot(pbuf[:, cols].astype(BF16), pw_ref[gi], preferred_element_type=F32)
        mixed.append(m * ps_ref[:, cols])
    ssq = sum(jnp.sum(m * m, axis=-1, keepdims=True) for m in mixed)
    scale = lax.rsqrt(ssq * (1.0 / C_C) + EPS)
    for gi, m in enumerate(mixed):
        cols = slice(gi * C_G, (gi + 1) * C_G)
        y = (m * scale * gc_ref[:, cols]).astype(yc_ref.dtype)
        for r in range(RUNS):
            yc_ref[r, :, cols] = y[r * tj:(r + 1) * tj]


def _local_mix(proj, conv_w, conv_b, ln_g, ln_b, pool_w, pool_scale, g_a, g_c):
    J = proj.shape[2]
    S = RUNS * J
    tj = min(MIX_TJ, J)
    hb = tj // HALO
    last_hb = J // HALO - 1
    conv_slabs, pool_slabs = 2 * C_A // LANES, C_C // LANES
    pool_blk = POOL_SLAB // pool_slabs
    cur_map = lambda i: (0, i, 0)
    const2 = lambda i: (0, 0)
    slab_specs = lambda n, first: [
        pl.BlockSpec((n, RUNS, tj, LANES), lambda i: (first, 0, i, 0)),
        pl.BlockSpec((n, RUNS, HALO, LANES), lambda i: (first, 0, jnp.maximum(i * hb - 1, 0), 0)),
        pl.BlockSpec((n, RUNS, HALO, LANES), lambda i: (first, 0, jnp.minimum((i + 1) * hb, last_hb), 0)),
    ]
    ya, yc = pl.pallas_call(
        functools.partial(_local_mix_kernel, seq_len=S),
        grid=(J // tj,),
        in_specs=slab_specs(conv_slabs, 0) + slab_specs(pool_slabs, pool_blk) + [
            pl.BlockSpec((CONV_WIDTH, C_A), const2),
            pl.BlockSpec((1, C_A), const2),
            pl.BlockSpec((1, C_A), const2),
            pl.BlockSpec((1, C_A), const2),
            pl.BlockSpec((len(POOL_WINDOWS), C_G, C_G), lambda i: (0, 0, 0)),
            pl.BlockSpec((1, C_C), const2),
            pl.BlockSpec((1, C_A), const2),
            pl.BlockSpec((1, C_C), const2),
        ],
        out_specs=[pl.BlockSpec((RUNS, tj, C_A), cur_map), pl.BlockSpec((RUNS, tj, C_C), cur_map)],
        out_shape=[jax.ShapeDtypeStruct((RUNS, J, C_A), BF16),
                   jax.ShapeDtypeStruct((RUNS, J, C_C), BF16)],
        scratch_shapes=[
            pltpu.VMEM((RUNS, tj + 2 * HALO, C_A), F32),
            pltpu.VMEM((RUNS, tj, C_A), F32),
            pltpu.VMEM((RUNS, tj, C_A), F32),
            pltpu.VMEM((RUNS, tj, C_A), F32),
            pltpu.VMEM((RUNS, tj + 2 * HALO, C_C), F32),
            pltpu.VMEM((RUNS, tj, C_C), F32),
            pltpu.VMEM((RUNS, tj, C_C), F32),
            pltpu.VMEM((RUNS * tj, C_C), F32),
        ],
        compiler_params=pltpu.CompilerParams(
            dimension_semantics=("arbitrary",), vmem_limit_bytes=VMEM_LIMIT),
        name="local_mix",
    )(proj, proj, proj, proj, proj, proj,
      conv_w, conv_b, ln_g, ln_b, pool_w, pool_scale, g_a, g_c)
    return ya.reshape(S, C_A), yc.reshape(S, C_C)


ATT_TQ = 128
ATT_W = ATT_TQ + 2 * HALF_WINDOW
ATT_UNROLL = 8
INTERIOR, FIRST, LAST = 0, 1, 2


def _block_layout(d, kind):
    pieces = RUNS // d
    q_rows, k_rows = ATT_TQ // pieces, ATT_W // pieces
    first_group = {INTERIOR: -HALF_WINDOW // pieces, FIRST: 0, LAST: (ATT_TQ - ATT_W) // pieces}[kind]
    a_q, u_q = np.divmod(np.arange(ATT_TQ), q_rows)
    a_k, u_k = np.divmod(np.arange(ATT_W), k_rows)
    return pieces * u_q + a_q, pieces * (u_k + first_group) + a_k, first_group


def _band_bias():
    tables = []
    for d in DILATIONS:
        for kind in (INTERIOR, FIRST, LAST):
            q_step, k_step, _ = _block_layout(d, kind)
            ok = np.abs(k_step[None, :] - q_step[:, None]) <= HALF_WINDOW
            tables.append(np.where(ok, 0.0, NEG))
    return np.stack(tables).astype(np.float32)


def _attention_kernel(q_ref, k_ref, v_ref, bias_ref, o_ref, m_acc, s_acc):
    S = o_ref.shape[0]
    J = S // RUNS
    lane = lax.broadcasted_iota(jnp.int32, (ATT_TQ, LANES), 1)
    head_a = lane < HEAD_DIM
    nt_dims = (((1,), (1,)), ((), ()))

    def gather(ref, starts, rows):
        return jnp.concatenate([ref[pl.ds(s, rows), :] for s in starts], axis=0)

    def attend(q, kw, vw, bias):
        q2, kw, vw = q.astype(BF16), kw.astype(BF16), vw.astype(BF16)
        zero = jnp.zeros_like(q2)

        def one_head(qh):
            s = lax.dot_general(qh, kw, nt_dims, preferred_element_type=F32) + bias
            m = jnp.max(s, axis=-1, keepdims=True)
            p = jnp.exp(s - m)
            l = jnp.sum(p, axis=-1, keepdims=True)
            o = jnp.dot(p.astype(BF16), vw, preferred_element_type=F32)
            return m, l, o

        ma, la, oa = one_head(jnp.where(head_a, q2, zero))
        mb, lb, ob = one_head(jnp.where(head_a, zero, q2))
        return jnp.where(head_a, ma, mb), jnp.where(head_a, la, lb), jnp.where(head_a, oa, ob)

    def block(pattern, kind, q_starts, k_starts, bias):
        pieces = len(q_starts)
        q_rows, k_rows = ATT_TQ // pieces, ATT_W // pieces
        q = gather(q_ref.at[0], q_starts, q_rows)
        kw = gather(k_ref.at[0], k_starts, k_rows)
        vw = gather(v_ref.at[0], k_starts, k_rows)
        m_blk, l_blk, o_blk = attend(q, kw, vw, bias)
        if pattern > 0:
            m_old = gather(m_acc, q_starts, q_rows)
            m_new = jnp.maximum(m_old, m_blk)
            w_old = jnp.exp(m_old - m_new)
            w_blk = jnp.exp(m_blk - m_new)
            l_blk = w_old * gather(s_acc, q_starts, q_rows) + w_blk * l_blk
            o_blk = w_old * gather(o_ref, q_starts, q_rows) + w_blk * o_blk
            m_blk = m_new
        for a, s in enumerate(q_starts):
            rows = slice(a * q_rows, (a + 1) * q_rows)
            m_acc[pl.ds(s, q_rows), :] = m_blk[rows]
            s_acc[pl.ds(s, q_rows), :] = l_blk[rows]
            o_ref[pl.ds(s, q_rows), :] = o_blk[rows]

    def aligned(x):
        if isinstance(x, int):
            assert x % SUBLANES == 0
            return x
        return pl.multiple_of(x, SUBLANES)

    for pattern, d in enumerate(DILATIONS):
        pieces = RUNS // d
        q_rows, k_rows = ATT_TQ // pieces, ATT_W // pieces
        n_blk = (S // d) // ATT_TQ
        assert n_blk >= 2, "first and last key blocks must be distinct"
        first_groups = [_block_layout(d, kind)[2] for kind in (INTERIOR, FIRST, LAST)]

        def run_block(sub, jb, kind, pattern=pattern, d=d, pieces=pieces, q_rows=q_rows,
                      first_groups=first_groups):
            g0 = jb * q_rows
            q_starts = [aligned((sub + d * a) * J + g0) for a in range(pieces)]
            if kind == INTERIOR and first_groups[kind] % SUBLANES:
                lead = SUBLANES - (-first_groups[kind]) % SUBLANES
                k_starts = [aligned((sub + d * a) * J + g0 - lead + first_groups[kind]) + lead
                            for a in range(pieces)]
            else:
                k_starts = [aligned((sub + d * a) * J + g0 + first_groups[kind]) for a in range(pieces)]
            block(pattern, kind, q_starts, k_starts, bias_ref[3 * pattern + kind])

        def kind_of(jb, n_blk=n_blk):
            return FIRST if jb == 0 else LAST if jb == n_blk - 1 else INTERIOR

        group = ATT_UNROLL
        if n_blk <= group:
            subs_per_trip = group // n_blk

            def short_subs(t, carry, run_block=run_block, n_blk=n_blk, subs_per_trip=subs_per_trip,
                           kind_of=kind_of):
                for k in range(subs_per_trip):
                    for jb in range(n_blk):
                        run_block(t * subs_per_trip + k, jb, kind_of(jb))
                return carry

            assert d % subs_per_trip == 0
            lax.fori_loop(0, d // subs_per_trip, short_subs, 0)
        else:
            n_mid = n_blk - 2 * group
            assert n_mid >= 0 and n_mid % group == 0

            def long_sub(sub, carry, run_block=run_block, n_blk=n_blk, n_mid=n_mid, kind_of=kind_of,
                         group=group):
                for jb in range(group):
                    run_block(sub, jb, kind_of(jb))

                def middle(t, c):
                    for k in range(group):
                        run_block(sub, group + t * group + k, INTERIOR)
                    return c

                if n_mid:
                    lax.fori_loop(0, n_mid // group, middle, 0)
                for jb in range(n_blk - group, n_blk):
                    run_block(sub, jb, kind_of(jb))
                return carry

            if d == 1:
                long_sub(0, 0)
            else:
                lax.fori_loop(0, d, long_sub, 0)

    def normalise(c, carry):
        rows = pl.ds(pl.multiple_of(c * ATT_W, ATT_W), ATT_W)
        o_ref[rows, :] = o_ref[rows, :] / s_acc[rows, :]
        return carry

    lax.fori_loop(0, S // ATT_W, normalise, 0)


def _attention(proj, bias):
    S = proj.shape[1] * proj.shape[2]
    qkv = proj.reshape(N_SLABS, S, LANES)
    qkv_spec = lambda off: pl.BlockSpec((1, S, LANES), lambda p: (p + off, 0, 0))
    return pl.pallas_call(
        _attention_kernel,
        grid=(N_PAIRS,),
        in_specs=[
            qkv_spec(Q_SLAB), qkv_spec(K_SLAB), qkv_spec(V_SLAB),
            pl.BlockSpec(bias.shape, lambda p: (0, 0, 0)),
        ],
        out_specs=pl.BlockSpec((S, LANES), lambda p: (0, p)),
        out_shape=jax.ShapeDtypeStruct((S, C_B), F32),
        scratch_shapes=[pltpu.VMEM((S, LANES), F32), pltpu.VMEM((S, LANES), F32)],
        compiler_params=pltpu.CompilerParams(
            dimension_semantics=("arbitrary",), vmem_limit_bytes=VMEM_LIMIT),
        name="attention",
    )(qkv, qkv, qkv, bias)


def _out_proj_kernel(ya_ref, yb_ref, yc_ref, gb_ref, w_ref, x_ref, gp_ref, o_ref, *by_run, x_is_natural):
    yb = _load_rows(yb_ref)
    ybn = (yb * _rms_scale(yb) * gb_ref[...]).astype(BF16)
    acc = jnp.dot(_load_rows(ya_ref), w_ref[0:C_A, :], preferred_element_type=F32)
    acc = acc + jnp.dot(ybn, w_ref[C_A:C_A + C_B, :], preferred_element_type=F32)
    acc = acc + jnp.dot(_load_rows(yc_ref), w_ref[C_A + C_B:, :], preferred_element_type=F32)
    update = acc * _rms_scale(acc) * gp_ref[...]
    if x_is_natural:
        rows = update.shape[0] // RUNS
        _regroup_by_run(x_ref, by_run[0])
        for r in range(RUNS):
            o_ref[r] = by_run[0][r, 0:rows, :] + update[r * rows:(r + 1) * rows]
    else:
        o_ref[...] = x_ref[...] + update


def _out_proj(ya, yb, yc, g_b, w, layer, x, x_is_natural, g_post):
    S = ya.shape[0]
    tm = _row_tile(S)
    const2 = lambda i: (0, 0)
    ya_view, ya_spec = _rows_spec(S, C_A, mixed=x_is_natural)
    yb_view, yb_spec = _rows_spec(S, C_B, mixed=x_is_natural)
    yc_view, yc_spec = _rows_spec(S, C_C, mixed=x_is_natural)
    out_view, out_spec = _rows_spec(S, D_MODEL, mixed=x_is_natural)
    x_view, x_spec = _natural_spec(S, D_MODEL) if x_is_natural else (out_view, out_spec)
    out = pl.pallas_call(
        functools.partial(_out_proj_kernel, x_is_natural=x_is_natural),
        grid=(S // tm,),
        in_specs=[
            ya_spec, yb_spec, yc_spec,
            pl.BlockSpec((1, C_B), const2),
            pl.BlockSpec((None, MIX_WIDTH, D_MODEL), lambda i: (layer, 0, 0)),
            x_spec,
            pl.BlockSpec((1, D_MODEL), const2),
        ],
        out_specs=out_spec,
        out_shape=jax.ShapeDtypeStruct(out_view, F32),
        scratch_shapes=(
            [pltpu.VMEM((RUNS, _regroup_pitch(tm // RUNS), D_MODEL), F32)] if x_is_natural else []),
        compiler_params=pltpu.CompilerParams(
            dimension_semantics=("arbitrary",), vmem_limit_bytes=VMEM_LIMIT),
        name="out_proj",
    )(ya.reshape(ya_view), yb.reshape(yb_view), yc.reshape(yc_view), g_b, w, x.reshape(x_view), g_post)
    return out.reshape(S, D_MODEL)


FFN_TH = 512


def _ffn_kernel(x_ref, gpre_ref, wg_ref, wu_ref, wd_ref, gpost_ref, o_ref, h_ref, acc_ref,
                *, out_is_natural):
    j = pl.program_id(1)

    @pl.when(j == 0)
    def _():
        x = _load_rows(x_ref)
        h_ref[...] = (x * _rms_scale(x) * gpre_ref[...]).astype(BF16)
        acc_ref[...] = jnp.zeros_like(acc_ref)

    h = h_ref[...]
    gate = jnp.dot(h, wg_ref[...], preferred_element_type=F32)
    up = jnp.dot(h, wu_ref[...], preferred_element_type=F32)
    act = (gate * jax.nn.sigmoid(gate) * up).astype(BF16)
    acc_ref[...] += jnp.dot(act, wd_ref[...], preferred_element_type=F32)

    @pl.when(j == pl.num_programs(1) - 1)
    def _():
        f = acc_ref[...]
        y = _load_rows(x_ref) + f * _rms_scale(f) * gpost_ref[...]
        if out_is_natural:
            rows = y.shape[0] // RUNS
            for r in range(RUNS):
                o_ref[:, r, :] = y[r * rows:(r + 1) * rows]
        else:
            o_ref[...] = y


def _ffn(x, g_pre, w_gate, w_up, w_down, layer, g_post, out_is_natural):
    S = x.shape[0]
    tm = _row_tile(S)
    x_view, x_spec = _rows_spec(S, D_MODEL, mixed=out_is_natural)
    out_view, out_spec = _natural_spec(S, D_MODEL) if out_is_natural else (x_view, x_spec)
    out = pl.pallas_call(
        functools.partial(_ffn_kernel, out_is_natural=out_is_natural),
        grid=(S // tm, FFN_HIDDEN // FFN_TH),
        in_specs=[
            x_spec,
            pl.BlockSpec((1, D_MODEL), lambda i, j: (0, 0)),
            pl.BlockSpec((None, D_MODEL, FFN_TH), lambda i, j: (layer, 0, j)),
            pl.BlockSpec((None, D_MODEL, FFN_TH), lambda i, j: (layer, 0, j)),
            pl.BlockSpec((None, FFN_TH, D_MODEL), lambda i, j: (layer, j, 0)),
            pl.BlockSpec((1, D_MODEL), lambda i, j: (0, 0)),
        ],
        out_specs=out_spec,
        out_shape=jax.ShapeDtypeStruct(out_view, F32),
        scratch_shapes=[pltpu.VMEM((tm, D_MODEL), BF16), pltpu.VMEM((tm, D_MODEL), F32)],
        compiler_params=pltpu.CompilerParams(
            dimension_semantics=("arbitrary", "arbitrary"), vmem_limit_bytes=VMEM_LIMIT),
        name="ffn",
    )(x.reshape(x_view), g_pre, w_gate, w_up, w_down, g_post)
    return out.reshape(S, D_MODEL)


def _rope_lane_tables(S):
    half = ROT_DIM // 2
    J = S // RUNS
    pos = (np.arange(J)[None, :] * RUNS + np.arange(RUNS)[:, None]).reshape(S).astype(np.float64)
    inv = ROPE_THETA ** (-np.arange(0, ROT_DIM, 2, dtype=np.float64) / ROT_DIM)
    ang = pos[:, None] * inv[None, :]
    cos, sin = np.cos(ang), np.sin(ang)
    ones = np.ones((S, HEAD_DIM - ROT_DIM))
    zeros_h = np.zeros((S, half))
    zeros_r = np.zeros((S, HEAD_DIM - ROT_DIM))
    cos_h = np.concatenate([cos, cos, ones], axis=1)
    sin_lo_h = np.concatenate([-sin, zeros_h, zeros_r], axis=1)
    sin_hi_h = np.concatenate([zeros_h, sin, zeros_r], axis=1)
    pair = lambda t: jnp.asarray(np.concatenate([t, t], axis=1).astype(np.float32))
    return pair(cos_h), pair(sin_lo_h), pair(sin_hi_h)


def kernel(x, w_in, conv_w, conv_b, conv_ln_g, conv_ln_b, pool_w, pool_scale, g_mix, w_out,
           g_pre_mix, g_post_mix, g_pre_ffn, g_post_ffn, w_gate, w_up, w_down):
    B, S, _ = x.shape
    depth = w_in.shape[0]
    assert B == 1, "the sequence-mixing kernels treat all rows as one sequence"
    cos_t, sin_lo_t, sin_hi_t = _rope_lane_tables(S)
    bias = jnp.asarray(_band_bias())
    row = lambda v: v.reshape(1, -1)
    w_in, w_out, w_gate, w_up, w_down, pool_w = (
        t.astype(BF16) for t in (w_in, w_out, w_gate, w_up, w_down, pool_w))

    xb = x.reshape(S, D_MODEL)
    for l in range(depth):
        natural = l == 0
        proj = _in_proj(xb, natural, row(g_pre_mix[l]), w_in, l, cos_t, sin_lo_t, sin_hi_t)
        ya, yc = _local_mix(proj, conv_w[l], row(conv_b[l]), row(conv_ln_g[l]),
                            row(conv_ln_b[l]), pool_w[l], row(pool_scale[l]),
                            row(g_mix[l, :C_A]), row(g_mix[l, C_A + C_B:]))
        yb = _attention(proj, bias)
        xb = _out_proj(ya, yb, yc, row(g_mix[l, C_A:C_A + C_B]), w_out, l, xb, natural, row(g_post_mix[l]))
        xb = _ffn(xb, row(g_pre_ffn[l]), w_gate, w_up, w_down, l, row(g_post_ffn[l]),
                  out_is_natural=l == depth - 1)
    return xb.reshape(B, S, D_MODEL)
```

```python
import functools

import numpy as np
import jax
import jax.numpy as jnp
from jax import lax
from jax.experimental import pallas as pl
from jax.experimental.pallas import tpu as pltpu

F32 = jnp.float32
BF16 = jnp.bfloat16

D_MODEL = 2048
C_A = 512
N_HEADS = 16
HEAD_DIM = 64
C_B = N_HEADS * HEAD_DIM
C_C = 512
POOL_WINDOWS = (2, 4, 8, 16)
C_G = C_C // len(POOL_WINDOWS)
MIX_WIDTH = C_A + C_B + C_C
IN_WIDTH = 2 * C_A + 3 * C_B + C_C
CONV_WIDTH = 31
CONV_HALF = CONV_WIDTH // 2
DILATIONS = (1, 4, 16)
HALF_WINDOW = 64
ROT_DIM = HEAD_DIM // 4
ROPE_THETA = 500000.0
FFN_HIDDEN = 5632
EPS = 1e-6
NEG = -1e30

LANES = 128
SUBLANES = 8
N_PAIRS = N_HEADS // 2
RUNS = DILATIONS[-1]
ROW_TILE = 512
VMEM_LIMIT = 52 * 1024 * 1024


def _rms_scale(t):
    return lax.rsqrt(jnp.mean(t * t, axis=-1, keepdims=True) + EPS)


def _row_tile(S):
    tm = min(ROW_TILE, S // RUNS)
    assert (S // RUNS) % tm == 0 and tm % (RUNS * 2 * SUBLANES) == 0
    return tm


def _rows_spec(S, width, mixed, tm=None, **spec_kwargs):
    J, tm = S // RUNS, tm or _row_tile(S)
    if mixed:
        return (RUNS, J, width), pl.BlockSpec((RUNS, tm // RUNS, width), lambda i, *_: (0, i, 0),
                                              **spec_kwargs)
    return (S, width), pl.BlockSpec((tm, width), lambda i, *_: (i, 0), **spec_kwargs)


def _natural_spec(S, width, tm=None):
    J, tm = S // RUNS, tm or _row_tile(S)
    return (J, RUNS, width), pl.BlockSpec((tm // RUNS, RUNS, width), lambda i, *_: (i, 0, 0))


def _load_rows(ref):
    v = ref[...]
    return v.reshape(-1, v.shape[-1]) if v.ndim == 3 else v


def _regroup_pitch(rows):
    return rows + SUBLANES if (rows // SUBLANES) % 2 == 0 else rows


def _regroup_by_run(x_ref, by_run_ref):
    for j in range(x_ref.shape[0]):
        by_run_ref[:, j, :] = x_ref[j]


IN_TN = 1536
IN_CHUNK = 256
N_SLABS = IN_WIDTH // LANES
Q_SLAB, K_SLAB, V_SLAB, POOL_SLAB = 8, 16, 24, 32


def _in_proj_kernel(x_ref, g_ref, w_ref, cos_ref, sin_lo_ref, sin_hi_ref, proj_ref, h_ref,
                    *by_run, x_is_natural):
    j = pl.program_id(1)
    tm = h_ref.shape[0]

    @pl.when(j == 0)
    def _():
        if x_is_natural:
            rows = tm // RUNS
            _regroup_by_run(x_ref, by_run[0])
            for r in range(RUNS):
                x = by_run[0][r, 0:rows, :]
                h_ref[r * rows:(r + 1) * rows, :] = (x * _rms_scale(x) * g_ref[...]).astype(BF16)
        else:
            x = x_ref[...]
            h_ref[...] = (x * _rms_scale(x) * g_ref[...]).astype(BF16)

    slabs_per_step = IN_TN // LANES
    for step in range(IN_WIDTH // IN_TN):
        @pl.when(j == step)
        def _(step=step):
            for c in range(IN_TN // IN_CHUNK):
                acc = jnp.dot(h_ref[...], w_ref[:, c * IN_CHUNK:(c + 1) * IN_CHUNK],
                              preferred_element_type=F32)
                for t in range(IN_CHUNK // LANES):
                    local = c * (IN_CHUNK // LANES) + t
                    slab = step * slabs_per_step + local
                    blk = acc[:, t * LANES:(t + 1) * LANES]
                    if Q_SLAB <= slab < V_SLAB:
                        blk = (blk * _load_rows(cos_ref)
                               + pltpu.roll(blk, ROT_DIM // 2, 1) * _load_rows(sin_hi_ref)
                               + pltpu.roll(blk, LANES - ROT_DIM // 2, 1) * _load_rows(sin_lo_ref))
                        if slab < K_SLAB:
                            blk = blk * HEAD_DIM ** -0.5
                    proj_ref[local] = blk.reshape(proj_ref.shape[1:])


def _in_proj(x, x_is_natural, g, w, layer, cos_t, sin_lo_t, sin_hi_t):
    S = x.shape[0]
    J, tm = S // RUNS, _row_tile(S)
    nj = J // tm
    slabs_per_step = IN_TN // LANES
    if x_is_natural:
        view, x_spec = _natural_spec(S, D_MODEL)
        tab_view, tab_spec = _rows_spec(S, LANES, mixed=True)
        out_spec = pl.BlockSpec((slabs_per_step, RUNS, tm // RUNS, LANES), lambda i, j: (j, 0, i, 0))
    else:
        view, x_spec = _rows_spec(S, D_MODEL, mixed=False)
        tab_view, tab_spec = _rows_spec(S, LANES, mixed=False)
        out_spec = pl.BlockSpec((slabs_per_step, None, tm, LANES), lambda i, j: (j, i // nj, i % nj, 0))
    tables = [t.reshape(tab_view) for t in (cos_t, sin_lo_t, sin_hi_t)]
    return pl.pallas_call(
        functools.partial(_in_proj_kernel, x_is_natural=x_is_natural),
        grid=(S // tm, IN_WIDTH // IN_TN),
        in_specs=[
            x_spec,
            pl.BlockSpec((1, D_MODEL), lambda i, j: (0, 0)),
            pl.BlockSpec((None, D_MODEL, IN_TN), lambda i, j: (layer, 0, j)),
            tab_spec, tab_spec, tab_spec,
        ],
        out_specs=out_spec,
        out_shape=jax.ShapeDtypeStruct((N_SLABS, RUNS, J, LANES), F32),
        scratch_shapes=[pltpu.VMEM((tm, D_MODEL), BF16)] + (
            [pltpu.VMEM((RUNS, _regroup_pitch(tm // RUNS), D_MODEL), F32)] if x_is_natural else []),
        compiler_params=pltpu.CompilerParams(
            dimension_semantics=("arbitrary", "arbitrary"), vmem_limit_bytes=VMEM_LIMIT),
        name="in_proj",
    )(x.reshape(view), g, w, *tables)


MIX_TJ = 32
HALO = SUBLANES


def _tap_source(r, offset):
    q = r + offset
    return q % RUNS, q // RUNS


def _local_mix_kernel(cu_ref, cu_prev_ref, cu_next_ref, pu_ref, pu_prev_ref, pu_next_ref,
                      cw_ref, cb_ref, lng_ref, lnb_ref, pw_ref, ps_ref, ga_ref, gc_ref,
                      ya_ref, yc_ref, hbuf, h_dn, h_up, cbuf, ubuf, u_dn, u_up, pbuf, *, seq_len):
    i = pl.program_id(0)
    n = pl.num_programs(0)
    tj = ya_ref.shape[1]
    not_first = (i > 0).astype(F32)
    not_last = (i < n - 1).astype(F32)
    mid = slice(HALO, HALO + tj)

    def glu(ref, lc, r):
        return ref[lc, r] * jax.nn.sigmoid(ref[C_A // LANES + lc, r])

    n_lc = C_A // LANES
    for r in range(RUNS):
        for lc in range(n_lc):
            cols = slice(lc * LANES, (lc + 1) * LANES)
            hbuf[r, 0:HALO, cols] = glu(cu_prev_ref, lc, r) * not_first
            hbuf[r, mid, cols] = glu(cu_ref, lc, r)
            hbuf[r, HALO + tj:, cols] = glu(cu_next_ref, lc, r) * not_last
        for gi in range(len(POOL_WINDOWS)):
            cols = slice(gi * C_G, (gi + 1) * C_G)
            ubuf[r, 0:HALO, cols] = pu_prev_ref[gi, r] * not_first
            ubuf[r, mid, cols] = pu_ref[gi, r]
            ubuf[r, HALO + tj:, cols] = pu_next_ref[gi, r] * not_last
    for r in range(RUNS):
        h_dn[r] = hbuf[r, HALO - 1:HALO - 1 + tj, :]
        h_up[r] = hbuf[r, HALO + 1:HALO + 1 + tj, :]
        u_dn[r] = ubuf[r, HALO - 1:HALO - 1 + tj, :]
        u_up[r] = ubuf[r, HALO + 1:HALO + 1 + tj, :]

    def shifted(center, dn, up, run, shift, cols):
        if shift == 0:
            return center[run, mid, cols]
        return (dn if shift < 0 else up)[run, :, cols]

    for r in range(RUNS):
        for lc in range(C_A // LANES):
            cols = slice(lc * LANES, (lc + 1) * LANES)
            acc = jnp.zeros((tj, LANES), F32)
            for t in range(CONV_WIDTH):
                run, shift = _tap_source(r, t - CONV_HALF)
                acc = acc + shifted(hbuf, h_dn, h_up, run, shift, cols) * cw_ref[t:t + 1, cols]
            cbuf[r, :, cols] = acc + cb_ref[:, cols]

    for r in range(RUNS):
        c = cbuf[r]
        mu = jnp.mean(c, axis=-1, keepdims=True)
        cc = c - mu
        var = jnp.mean(cc * cc, axis=-1, keepdims=True)
        y = cc * lax.rsqrt(var + EPS) * lng_ref[...] + lnb_ref[...]
        y = y * jax.nn.sigmoid(y)
        ya_ref[r] = (y * _rms_scale(y) * ga_ref[...]).astype(ya_ref.dtype)

    row = lax.broadcasted_iota(jnp.int32, (tj, LANES), 0)
    for r in range(RUNS):
        pos = (i * tj + row) * RUNS + r
        for gi, win in enumerate(POOL_WINDOWS):
            cols = slice(gi * C_G, (gi + 1) * C_G)
            half = win // 2
            tot = jnp.zeros((tj, LANES), F32)
            for o in range(-half, win - half):
                run, shift = _tap_source(r, o)
                tot = tot + shifted(ubuf, u_dn, u_up, run, shift, cols)
            lo = jnp.maximum(pos - half, 0)
            hi = jnp.minimum(pos + (win - half), seq_len)
            pbuf[r * tj:(r + 1) * tj, cols] = tot / (hi - lo).astype(F32) - ubuf[r, mid, cols]

    mixed = []
    for gi in range(len(POOL_WINDOWS)):
        cols = slice(gi * C_G, (gi + 1) * C_G)
        m = jnp.dot(pbuf[:, cols].astype(BF16), pw_ref[gi], preferred_element_type=F32)
        mixed.append(m * ps_ref[:, cols])
    ssq = sum(jnp.sum(m * m, axis=-1, keepdims=True) for m in mixed)
    scale = lax.rsqrt(ssq * (1.0 / C_C) + EPS)
    for gi, m in enumerate(mixed):
        cols = slice(gi * C_G, (gi + 1) * C_G)
        y = (m * scale * gc_ref[:, cols]).astype(yc_ref.dtype)
        for r in range(RUNS):
            yc_ref[r, :, cols] = y[r * tj:(r + 1) * tj]


def _local_mix(proj, conv_w, conv_b, ln_g, ln_b, pool_w, pool_scale, g_a, g_c):
    J = proj.shape[2]
    S = RUNS * J
    tj = min(MIX_TJ, J)
    hb = tj // HALO
    last_hb = J // HALO - 1
    conv_slabs, pool_slabs = 2 * C_A // LANES, C_C // LANES
    pool_blk = POOL_SLAB // pool_slabs
    cur_map = lambda i: (0, i, 0)
    const2 = lambda i: (0, 0)
    slab_specs = lambda n, first: [
        pl.BlockSpec((n, RUNS, tj, LANES), lambda i: (first, 0, i, 0)),
        pl.BlockSpec((n, RUNS, HALO, LANES), lambda i: (first, 0, jnp.maximum(i * hb - 1, 0), 0)),
        pl.BlockSpec((n, RUNS, HALO, LANES), lambda i: (first, 0, jnp.minimum((i + 1) * hb, last_hb), 0)),
    ]
    ya, yc = pl.pallas_call(
        functools.partial(_local_mix_kernel, seq_len=S),
        grid=(J // tj,),
        in_specs=slab_specs(conv_slabs, 0) + slab_specs(pool_slabs, pool_blk) + [
            pl.BlockSpec((CONV_WIDTH, C_A), const2),
            pl.BlockSpec((1, C_A), const2),
            pl.BlockSpec((1, C_A), const2),
            pl.BlockSpec((1, C_A), const2),
            pl.BlockSpec((len(POOL_WINDOWS), C_G, C_G), lambda i: (0, 0, 0)),
            pl.BlockSpec((1, C_C), const2),
            pl.BlockSpec((1, C_A), const2),
            pl.BlockSpec((1, C_C), const2),
        ],
        out_specs=[pl.BlockSpec((RUNS, tj, C_A), cur_map), pl.BlockSpec((RUNS, tj, C_C), cur_map)],
        out_shape=[jax.ShapeDtypeStruct((RUNS, J, C_A), BF16),
                   jax.ShapeDtypeStruct((RUNS, J, C_C), BF16)],
        scratch_shapes=[
            pltpu.VMEM((RUNS, tj + 2 * HALO, C_A), F32),
            pltpu.VMEM((RUNS, tj, C_A), F32),
            pltpu.VMEM((RUNS, tj, C_A), F32),
            pltpu.VMEM((RUNS, tj, C_A), F32),
            pltpu.VMEM((RUNS, tj + 2 * HALO, C_C), F32),
            pltpu.VMEM((RUNS, tj, C_C), F32),
            pltpu.VMEM((RUNS, tj, C_C), F32),
            pltpu.VMEM((RUNS * tj, C_C), F32),
        ],
        compiler_params=pltpu.CompilerParams(
            dimension_semantics=("arbitrary",), vmem_limit_bytes=VMEM_LIMIT),
        name="local_mix",
    )(proj, proj, proj, proj, proj, proj,
      conv_w, conv_b, ln_g, ln_b, pool_w, pool_scale, g_a, g_c)
    return ya.reshape(S, C_A), yc.reshape(S, C_C)


ATT_TQ = 128
ATT_W = ATT_TQ + 2 * HALF_WINDOW
ATT_UNROLL = 16
INTERIOR, FIRST, LAST = 0, 1, 2


def _block_layout(d, kind):
    pieces = RUNS // d
    q_rows, k_rows = ATT_TQ // pieces, ATT_W // pieces
    first_group = {INTERIOR: -HALF_WINDOW // pieces, FIRST: 0, LAST: (ATT_TQ - ATT_W) // pieces}[kind]
    a_q, u_q = np.divmod(np.arange(ATT_TQ), q_rows)
    a_k, u_k = np.divmod(np.arange(ATT_W), k_rows)
    return pieces * u_q + a_q, pieces * (u_k + first_group) + a_k, first_group


def _band_bias():
    tables = []
    for d in DILATIONS:
        for kind in (INTERIOR, FIRST, LAST):
            q_step, k_step, _ = _block_layout(d, kind)
            ok = np.abs(k_step[None, :] - q_step[:, None]) <= HALF_WINDOW
            tables.append(np.where(ok, 0.0, NEG))
    return np.stack(tables).astype(np.float32)


def _attention_kernel(q_ref, k_ref, v_ref, bias_ref, o_ref, m_acc, s_acc):
    S = o_ref.shape[0]
    J = S // RUNS
    lane = lax.broadcasted_iota(jnp.int32, (ATT_TQ, LANES), 1)
    head_a = lane < HEAD_DIM
    nt_dims = (((1,), (1,)), ((), ()))

    def gather(ref, starts, rows):
        return jnp.concatenate([ref[pl.ds(s, rows), :] for s in starts], axis=0)

    def attend(q, kw, vw, bias):
        q2, kw, vw = q.astype(BF16), kw.astype(BF16), vw.astype(BF16)
        zero = jnp.zeros_like(q2)

        def one_head(qh):
            s = lax.dot_general(qh, kw, nt_dims, preferred_element_type=F32) + bias
            m = jnp.max(s, axis=-1, keepdims=True)
            p = jnp.exp(s - m)
            l = jnp.sum(p, axis=-1, keepdims=True)
            o = jnp.dot(p.astype(BF16), vw, preferred_element_type=F32)
            return m, l, o

        ma, la, oa = one_head(jnp.where(head_a, q2, zero))
        mb, lb, ob = one_head(jnp.where(head_a, zero, q2))
        return jnp.where(head_a, ma, mb), jnp.where(head_a, la, lb), jnp.where(head_a, oa, ob)

    def block(pattern, kind, q_starts, k_starts, bias):
        pieces = len(q_starts)
        q_rows, k_rows = ATT_TQ // pieces, ATT_W // pieces
        q = gather(q_ref.at[0], q_starts, q_rows)
        kw = gather(k_ref.at[0], k_starts, k_rows)
        vw = gather(v_ref.at[0], k_starts, k_rows)
        m_blk, l_blk, o_blk = attend(q, kw, vw, bias)
        if pattern > 0:
            m_old = gather(m_acc, q_starts, q_rows)
            m_new = jnp.maximum(m_old, m_blk)
            w_old = jnp.exp(m_old - m_new)
            w_blk = jnp.exp(m_blk - m_new)
            l_blk = w_old * gather(s_acc, q_starts, q_rows) + w_blk * l_blk
            o_blk = w_old * gather(o_ref, q_starts, q_rows) + w_blk * o_blk
            m_blk = m_new
        for a, s in enumerate(q_starts):
            rows = slice(a * q_rows, (a + 1) * q_rows)
            m_acc[pl.ds(s, q_rows), :] = m_blk[rows]
            s_acc[pl.ds(s, q_rows), :] = l_blk[rows]
            o_ref[pl.ds(s, q_rows), :] = o_blk[rows]

    def aligned(x):
        if isinstance(x, int):
            assert x % SUBLANES == 0
            return x
        return pl.multiple_of(x, SUBLANES)

    for pattern, d in enumerate(DILATIONS):
        pieces = RUNS // d
        q_rows, k_rows = ATT_TQ // pieces, ATT_W // pieces
        n_blk = (S // d) // ATT_TQ
        assert n_blk >= 2, "first and last key blocks must be distinct"
        first_groups = [_block_layout(d, kind)[2] for kind in (INTERIOR, FIRST, LAST)]

        def run_block(sub, jb, kind, pattern=pattern, d=d, pieces=pieces, q_rows=q_rows,
                      first_groups=first_groups):
            g0 = jb * q_rows
            q_starts = [aligned((sub + d * a) * J + g0) for a in range(pieces)]
            if kind == INTERIOR and first_groups[kind] % SUBLANES:
                lead = SUBLANES - (-first_groups[kind]) % SUBLANES
                k_starts = [aligned((sub + d * a) * J + g0 - lead + first_groups[kind]) + lead
                            for a in range(pieces)]
            else:
                k_starts = [aligned((sub + d * a) * J + g0 + first_groups[kind]) for a in range(pieces)]
            block(pattern, kind, q_starts, k_starts, bias_ref[3 * pattern + kind])

        def kind_of(jb, n_blk=n_blk):
            return FIRST if jb == 0 else LAST if jb == n_blk - 1 else INTERIOR

        group = ATT_UNROLL
        if n_blk <= group:
            subs_per_trip = group // n_blk

            def short_subs(t, carry, run_block=run_block, n_blk=n_blk, subs_per_trip=subs_per_trip,
                           kind_of=kind_of):
                for k in range(subs_per_trip):
                    for jb in range(n_blk):
                        run_block(t * subs_per_trip + k, jb, kind_of(jb))
                return carry

            assert d % subs_per_trip == 0
            lax.fori_loop(0, d // subs_per_trip, short_subs, 0)
        else:
            n_mid = n_blk - 2 * group
            assert n_mid >= 0 and n_mid % group == 0

            def long_sub(sub, carry, run_block=run_block, n_blk=n_blk, n_mid=n_mid, kind_of=kind_of,
                         group=group):
                for jb in range(group):
                    run_block(sub, jb, kind_of(jb))

                def middle(t, c):
                    for k in range(group):
                        run_block(sub, group + t * group + k, INTERIOR)
                    return c

                if n_mid:
                    lax.fori_loop(0, n_mid // group, middle, 0)
                for jb in range(n_blk - group, n_blk):
                    run_block(sub, jb, kind_of(jb))
                return carry

            if d == 1:
                long_sub(0, 0)
            else:
                lax.fori_loop(0, d, long_sub, 0)

    def normalise(c, carry):
        rows = pl.ds(pl.multiple_of(c * ATT_W, ATT_W), ATT_W)
        o_ref[rows, :] = o_ref[rows, :] / s_acc[rows, :]
        return carry

    lax.fori_loop(0, S // ATT_W, normalise, 0)


def _attention(proj, bias):
    S = proj.shape[1] * proj.shape[2]
    qkv = proj.reshape(N_SLABS, S, LANES)
    qkv_spec = lambda off: pl.BlockSpec((1, S, LANES), lambda p: (p + off, 0, 0))
    return pl.pallas_call(
        _attention_kernel,
        grid=(N_PAIRS,),
        in_specs=[
            qkv_spec(Q_SLAB), qkv_spec(K_SLAB), qkv_spec(V_SLAB),
            pl.BlockSpec(bias.shape, lambda p: (0, 0, 0)),
        ],
        out_specs=pl.BlockSpec((S, LANES), lambda p: (0, p)),
        out_shape=jax.ShapeDtypeStruct((S, C_B), F32),
        scratch_shapes=[pltpu.VMEM((S, LANES), F32), pltpu.VMEM((S, LANES), F32)],
        compiler_params=pltpu.CompilerParams(
            dimension_semantics=("arbitrary",), vmem_limit_bytes=VMEM_LIMIT),
        name="attention",
    )(qkv, qkv, qkv, bias)


def _out_proj_kernel(ya_ref, yb_ref, yc_ref, gb_ref, w_ref, x_ref, gp_ref, o_ref, *by_run, x_is_natural):
    yb = _load_rows(yb_ref)
    ybn = (yb * _rms_scale(yb) * gb_ref[...]).astype(BF16)
    acc = jnp.dot(_load_rows(ya_ref), w_ref[0:C_A, :], preferred_element_type=F32)
    acc = acc + jnp.dot(ybn, w_ref[C_A:C_A + C_B, :], preferred_element_type=F32)
    acc = acc + jnp.dot(_load_rows(yc_ref), w_ref[C_A + C_B:, :], preferred_element_type=F32)
    update = acc * _rms_scale(acc) * gp_ref[...]
    if x_is_natural:
        rows = update.shape[0] // RUNS
        _regroup_by_run(x_ref, by_run[0])
        for r in range(RUNS):
            o_ref[r] = by_run[0][r, 0:rows, :] + update[r * rows:(r + 1) * rows]
    else:
        o_ref[...] = x_ref[...] + update


def _out_proj(ya, yb, yc, g_b, w, layer, x, x_is_natural, g_post):
    S = ya.shape[0]
    tm = _row_tile(S)
    const2 = lambda i: (0, 0)
    ya_view, ya_spec = _rows_spec(S, C_A, mixed=x_is_natural)
    yb_view, yb_spec = _rows_spec(S, C_B, mixed=x_is_natural)
    yc_view, yc_spec = _rows_spec(S, C_C, mixed=x_is_natural)
    out_view, out_spec = _rows_spec(S, D_MODEL, mixed=x_is_natural)
    x_view, x_spec = _natural_spec(S, D_MODEL) if x_is_natural else (out_view, out_spec)
    out = pl.pallas_call(
        functools.partial(_out_proj_kernel, x_is_natural=x_is_natural),
        grid=(S // tm,),
        in_specs=[
            ya_spec, yb_spec, yc_spec,
            pl.BlockSpec((1, C_B), const2),
            pl.BlockSpec((None, MIX_WIDTH, D_MODEL), lambda i: (layer, 0, 0)),
            x_spec,
            pl.BlockSpec((1, D_MODEL), const2),
        ],
        out_specs=out_spec,
        out_shape=jax.ShapeDtypeStruct(out_view, F32),
        scratch_shapes=(
            [pltpu.VMEM((RUNS, _regroup_pitch(tm // RUNS), D_MODEL), F32)] if x_is_natural else []),
        compiler_params=pltpu.CompilerParams(
            dimension_semantics=("arbitrary",), vmem_limit_bytes=VMEM_LIMIT),
        name="out_proj",
    )(ya.reshape(ya_view), yb.reshape(yb_view), yc.reshape(yc_view), g_b, w, x.reshape(x_view), g_post)
    return out.reshape(S, D_MODEL)


FFN_TM = 1024
FFN_TH = 256
FFN_VMEM_LIMIT = 58 * 1024 * 1024


def _ffn_kernel(x_ref, gpre_ref, wg_ref, wu_ref, wd_ref, gpost_ref, o_ref, h_ref, *acc,
                out_is_natural):
    j = pl.program_id(1)
    acc_ref = acc[0] if out_is_natural else o_ref

    @pl.when(j == 0)
    def _():
        x = _load_rows(x_ref)
        h_ref[...] = (x * _rms_scale(x) * gpre_ref[...]).astype(BF16)
        acc_ref[...] = jnp.zeros_like(acc_ref)

    h = h_ref[...]
    gate = jnp.dot(h, wg_ref[...].astype(BF16), preferred_element_type=F32)
    up = jnp.dot(h, wu_ref[...].astype(BF16), preferred_element_type=F32)
    act = (gate * jax.nn.sigmoid(gate) * up).astype(BF16)
    acc_ref[...] += jnp.dot(act, wd_ref[...].astype(BF16), preferred_element_type=F32)

    @pl.when(j == pl.num_programs(1) - 1)
    def _():
        f = acc_ref[...]
        y = _load_rows(x_ref) + f * _rms_scale(f) * gpost_ref[...]
        if out_is_natural:
            rows = y.shape[0] // RUNS
            for r in range(RUNS):
                o_ref[:, r, :] = y[r * rows:(r + 1) * rows]
        else:
            o_ref[...] = y


def _ffn(x, g_pre, w_gate, w_up, w_down, layer, g_post, out_is_natural):
    S = x.shape[0]
    tm = min(FFN_TM, S // 2)
    if out_is_natural:
        x_view, x_spec = _rows_spec(S, D_MODEL, mixed=True, tm=tm, pipeline_mode=pl.Buffered(1))
        out_view, out_spec = _natural_spec(S, D_MODEL, tm=tm)
        acc_scratch = [pltpu.VMEM((tm, D_MODEL), F32)]
    else:
        x_view, x_spec = _rows_spec(S, D_MODEL, mixed=False, tm=tm)
        out_view, out_spec = x_view, x_spec
        acc_scratch = []
    out = pl.pallas_call(
        functools.partial(_ffn_kernel, out_is_natural=out_is_natural),
        grid=(S // tm, FFN_HIDDEN // FFN_TH),
        in_specs=[
            x_spec,
            pl.BlockSpec((1, D_MODEL), lambda i, j: (0, 0)),
            pl.BlockSpec((None, D_MODEL, FFN_TH), lambda i, j: (layer, 0, j)),
            pl.BlockSpec((None, D_MODEL, FFN_TH), lambda i, j: (layer, 0, j)),
            pl.BlockSpec((None, FFN_TH, D_MODEL), lambda i, j: (layer, j, 0)),
            pl.BlockSpec((1, D_MODEL), lambda i, j: (0, 0)),
        ],
        out_specs=out_spec,
        out_shape=jax.ShapeDtypeStruct(out_view, F32),
        scratch_shapes=[pltpu.VMEM((tm, D_MODEL), BF16)] + acc_scratch,
        compiler_params=pltpu.CompilerParams(
            dimension_semantics=("arbitrary", "arbitrary"), vmem_limit_bytes=FFN_VMEM_LIMIT),
        name="ffn",
    )(x.reshape(x_view), g_pre, w_gate, w_up, w_down, g_post)
    return out.reshape(S, D_MODEL)


def _rope_lane_tables(S):
    half = ROT_DIM // 2
    J = S // RUNS
    pos = (np.arange(J)[None, :] * RUNS + np.arange(RUNS)[:, None]).reshape(S).astype(np.float64)
    inv = ROPE_THETA ** (-np.arange(0, ROT_DIM, 2, dtype=np.float64) / ROT_DIM)
    ang = pos[:, None] * inv[None, :]
    cos, sin = np.cos(ang), np.sin(ang)
    ones = np.ones((S, HEAD_DIM - ROT_DIM))
    zeros_h = np.zeros((S, half))
    zeros_r = np.zeros((S, HEAD_DIM - ROT_DIM))
    cos_h = np.concatenate([cos, cos, ones], axis=1)
    sin_lo_h = np.concatenate([-sin, zeros_h, zeros_r], axis=1)
    sin_hi_h = np.concatenate([zeros_h, sin, zeros_r], axis=1)
    pair = lambda t: jnp.asarray(np.concatenate([t, t], axis=1).astype(np.float32))
    return pair(cos_h), pair(sin_lo_h), pair(sin_hi_h)


def kernel(x, w_in, conv_w, conv_b, conv_ln_g, conv_ln_b, pool_w, pool_scale, g_mix, w_out,
           g_pre_mix, g_post_mix, g_pre_ffn, g_post_ffn, w_gate, w_up, w_down):
    B, S, _ = x.shape
    depth = w_in.shape[0]
    assert B == 1, "the sequence-mixing kernels treat all rows as one sequence"
    cos_t, sin_lo_t, sin_hi_t = _rope_lane_tables(S)
    bias = jnp.asarray(_band_bias())
    row = lambda v: v.reshape(1, -1)
    w_in, w_out, pool_w = (t.astype(BF16) for t in (w_in, w_out, pool_w))

    xb = x.reshape(S, D_MODEL)
    for l in range(depth):
        natural = l == 0
        proj = _in_proj(xb, natural, row(g_pre_mix[l]), w_in, l, cos_t, sin_lo_t, sin_hi_t)
        ya, yc = _local_mix(proj, conv_w[l], row(conv_b[l]), row(conv_ln_g[l]),
                            row(conv_ln_b[l]), pool_w[l], row(pool_scale[l]),
                            row(g_mix[l, :C_A]), row(g_mix[l, C_A + C_B:]))
        yb = _attention(proj, bias)
        xb = _out_proj(ya, yb, yc, row(g_mix[l, C_A:C_A + C_B]), w_out, l, xb, natural, row(g_post_mix[l]))
        xb = _ffn(xb, row(g_pre_ffn[l]), w_gate, w_up, w_down, l, row(g_post_ffn[l]),
                  out_is_natural=l == depth - 1)
    return xb.reshape(B, S, D_MODEL)
```

```python
import functools

import numpy as np
import jax
import jax.numpy as jnp
from jax import lax
from jax.experimental import pallas as pl
from jax.experimental.pallas import tpu as pltpu

F32 = jnp.float32
BF16 = jnp.bfloat16

D_MODEL = 2048
C_A = 512
N_HEADS = 16
HEAD_DIM = 64
C_B = N_HEADS * HEAD_DIM
C_C = 512
POOL_WINDOWS = (2, 4, 8, 16)
C_G = C_C // len(POOL_WINDOWS)
MIX_WIDTH = C_A + C_B + C_C
IN_WIDTH = 2 * C_A + 3 * C_B + C_C
CONV_WIDTH = 31
CONV_HALF = CONV_WIDTH // 2
DILATIONS = (1, 4, 16)
HALF_WINDOW = 64
ROT_DIM = HEAD_DIM // 4
ROPE_THETA = 500000.0
FFN_HIDDEN = 5632
EPS = 1e-6
NEG = -1e30

LANES = 128
SUBLANES = 8
N_PAIRS = N_HEADS // 2
RUNS = DILATIONS[-1]
ROW_TILE = 512
VMEM_LIMIT = 52 * 1024 * 1024


def _rms_scale(t):
    return lax.rsqrt(jnp.mean(t * t, axis=-1, keepdims=True) + EPS)


def _row_tile(S):
    tm = min(ROW_TILE, S // RUNS)
    assert (S // RUNS) % tm == 0 and tm % (RUNS * 2 * SUBLANES) == 0
    return tm


def _rows_spec(S, width, mixed, tm=None):
    J, tm = S // RUNS, tm or _row_tile(S)
    if mixed:
        return (RUNS, J, width), pl.BlockSpec((RUNS, tm // RUNS, width), lambda i, *_: (0, i, 0))
    return (S, width), pl.BlockSpec((tm, width), lambda i, *_: (i, 0))


def _natural_spec(S, width):
    J, tm = S // RUNS, _row_tile(S)
    return (J, RUNS, width), pl.BlockSpec((tm // RUNS, RUNS, width), lambda i, *_: (i, 0, 0))


def _load_rows(ref):
    v = ref[...]
    return v.reshape(-1, v.shape[-1]) if v.ndim == 3 else v


def _regroup_pitch(rows):
    return rows + SUBLANES if (rows // SUBLANES) % 2 == 0 else rows


def _regroup_by_run(x_ref, by_run_ref):
    for j in range(x_ref.shape[0]):
        by_run_ref[:, j, :] = x_ref[j]


IN_TN = 1536
IN_CHUNK = 256
N_SLABS = IN_WIDTH // LANES
Q_SLAB, K_SLAB, V_SLAB, POOL_SLAB = 8, 16, 24, 32
IN_RUNS_PER_TILE = 2


def _in_proj_kernel(x_ref, g_ref, w_ref, cos_ref, sin_lo_ref, sin_hi_ref, proj_ref, h_ref,
                    *by_run, x_is_natural):
    j = pl.program_id(1)
    tm = h_ref.shape[0]

    @pl.when(j == 0)
    def _():
        if x_is_natural:
            rows = tm // RUNS
            _regroup_by_run(x_ref, by_run[0])
            for r in range(RUNS):
                x = by_run[0][r, 0:rows, :]
                h_ref[r * rows:(r + 1) * rows, :] = (x * _rms_scale(x) * g_ref[...]).astype(BF16)
        else:
            x = x_ref[...]
            h_ref[...] = (x * _rms_scale(x) * g_ref[...]).astype(BF16)

    slabs_per_step = IN_TN // LANES
    for step in range(IN_WIDTH // IN_TN):
        @pl.when(j == step)
        def _(step=step):
            for c in range(IN_TN // IN_CHUNK):
                acc = jnp.dot(h_ref[...], w_ref[:, c * IN_CHUNK:(c + 1) * IN_CHUNK],
                              preferred_element_type=F32)
                for t in range(IN_CHUNK // LANES):
                    local = c * (IN_CHUNK // LANES) + t
                    slab = step * slabs_per_step + local
                    blk = acc[:, t * LANES:(t + 1) * LANES]
                    if Q_SLAB <= slab < V_SLAB:
                        blk = (blk * _load_rows(cos_ref)
                               + pltpu.roll(blk, ROT_DIM // 2, 1) * _load_rows(sin_hi_ref)
                               + pltpu.roll(blk, LANES - ROT_DIM // 2, 1) * _load_rows(sin_lo_ref))
                        if slab < K_SLAB:
                            blk = blk * HEAD_DIM ** -0.5
                    proj_ref[local] = blk.reshape(proj_ref.shape[1:])


def _in_proj(x, x_is_natural, g, w, layer, cos_t, sin_lo_t, sin_hi_t):
    S = x.shape[0]
    J = S // RUNS
    slabs_per_step = IN_TN // LANES
    if x_is_natural:
        tm = _row_tile(S)
        view, x_spec = _natural_spec(S, D_MODEL)
        tab_view, tab_spec = _rows_spec(S, LANES, mixed=True)
        out_spec = pl.BlockSpec((slabs_per_step, RUNS, tm // RUNS, LANES), lambda i, j: (j, 0, i, 0))
    else:
        tm = IN_RUNS_PER_TILE * J
        view, x_spec = _rows_spec(S, D_MODEL, mixed=False, tm=tm)
        tab_view, tab_spec = _rows_spec(S, LANES, mixed=False, tm=tm)
        out_spec = pl.BlockSpec((slabs_per_step, IN_RUNS_PER_TILE, J, LANES), lambda i, j: (j, i, 0, 0))
    tables = [t.reshape(tab_view) for t in (cos_t, sin_lo_t, sin_hi_t)]
    return pl.pallas_call(
        functools.partial(_in_proj_kernel, x_is_natural=x_is_natural),
        grid=(S // tm, IN_WIDTH // IN_TN),
        in_specs=[
            x_spec,
            pl.BlockSpec((1, D_MODEL), lambda i, j: (0, 0)),
            pl.BlockSpec((None, D_MODEL, IN_TN), lambda i, j: (layer, 0, j)),
            tab_spec, tab_spec, tab_spec,
        ],
        out_specs=out_spec,
        out_shape=jax.ShapeDtypeStruct((N_SLABS, RUNS, J, LANES), F32),
        scratch_shapes=[pltpu.VMEM((tm, D_MODEL), BF16)] + (
            [pltpu.VMEM((RUNS, _regroup_pitch(tm // RUNS), D_MODEL), F32)] if x_is_natural else []),
        compiler_params=pltpu.CompilerParams(
            dimension_semantics=("arbitrary", "arbitrary"), vmem_limit_bytes=VMEM_LIMIT),
        name="in_proj",
    )(x.reshape(view), g, w, *tables)


MIX_TJ = 32
HALO = SUBLANES


def _tap_source(r, offset):
    q = r + offset
    return q % RUNS, q // RUNS


def _local_mix_kernel(cu_ref, cu_prev_ref, cu_next_ref, pu_ref, pu_prev_ref, pu_next_ref,
                      cw_ref, cb_ref, lng_ref, lnb_ref, pw_ref, ps_ref, ga_ref, gc_ref,
                      ya_ref, yc_ref, hbuf, h_dn, h_up, cbuf, ubuf, u_dn, u_up, pbuf, *, seq_len):
    i = pl.program_id(0)
    n = pl.num_programs(0)
    tj = ya_ref.shape[1]
    not_first = (i > 0).astype(F32)
    not_last = (i < n - 1).astype(F32)
    mid = slice(HALO, HALO + tj)

    def glu(ref, lc, r):
        return ref[lc, r] * jax.nn.sigmoid(ref[C_A // LANES + lc, r])

    n_lc = C_A // LANES
    for r in range(RUNS):
        for lc in range(n_lc):
            cols = slice(lc * LANES, (lc + 1) * LANES)
            hbuf[r, 0:HALO, cols] = glu(cu_prev_ref, lc, r) * not_first
            hbuf[r, mid, cols] = glu(cu_ref, lc, r)
            hbuf[r, HALO + tj:, cols] = glu(cu_next_ref, lc, r) * not_last
        for gi in range(len(POOL_WINDOWS)):
            cols = slice(gi * C_G, (gi + 1) * C_G)
            ubuf[r, 0:HALO, cols] = pu_prev_ref[gi, r] * not_first
            ubuf[r, mid, cols] = pu_ref[gi, r]
            ubuf[r, HALO + tj:, cols] = pu_next_ref[gi, r] * not_last
    for r in range(RUNS):
        h_dn[r] = hbuf[r, HALO - 1:HALO - 1 + tj, :]
        h_up[r] = hbuf[r, HALO + 1:HALO + 1 + tj, :]
        u_dn[r] = ubuf[r, HALO - 1:HALO - 1 + tj, :]
        u_up[r] = ubuf[r, HALO + 1:HALO + 1 + tj, :]

    def shifted(center, dn, up, run, shift, cols):
        if shift == 0:
            return center[run, mid, cols]
        return (dn if shift < 0 else up)[run, :, cols]

    for r in range(RUNS):
        for lc in range(C_A // LANES):
            cols = slice(lc * LANES, (lc + 1) * LANES)
            acc = jnp.zeros((tj, LANES), F32)
            for t in range(CONV_WIDTH):
                run, shift = _tap_source(r, t - CONV_HALF)
                acc = acc + shifted(hbuf, h_dn, h_up, run, shift, cols) * cw_ref[t:t + 1, cols]
            cbuf[r, :, cols] = acc + cb_ref[:, cols]

    for r in range(RUNS):
        c = cbuf[r]
        mu = jnp.mean(c, axis=-1, keepdims=True)
        cc = c - mu
        var = jnp.mean(cc * cc, axis=-1, keepdims=True)
        y = cc * lax.rsqrt(var + EPS) * lng_ref[...] + lnb_ref[...]
        y = y * jax.nn.sigmoid(y)
        ya_ref[r] = (y * _rms_scale(y) * ga_ref[...]).astype(ya_ref.dtype)

    row = lax.broadcasted_iota(jnp.int32, (tj, LANES), 0)
    for r in range(RUNS):
        pos = (i * tj + row) * RUNS + r
        for gi, win in enumerate(POOL_WINDOWS):
            cols = slice(gi * C_G, (gi + 1) * C_G)
            half = win // 2
            tot = jnp.zeros((tj, LANES), F32)
            for o in range(-half, win - half):
                run, shift = _tap_source(r, o)
                tot = tot + shifted(ubuf, u_dn, u_up, run, shift, cols)
            lo = jnp.maximum(pos - half, 0)
            hi = jnp.minimum(pos + (win - half), seq_len)
            pbuf[r * tj:(r + 1) * tj, cols] = tot / (hi - lo).astype(F32) - ubuf[r, mid, cols]

    mixed = []
    for gi in range(len(POOL_WINDOWS)):
        cols = slice(gi * C_G, (gi + 1) * C_G)
        m = jnp.dot(pbuf[:, cols].astype(BF16), pw_ref[gi], preferred_element_type=F32)
        mixed.append(m * ps_ref[:, cols])
    ssq = sum(jnp.sum(m * m, axis=-1, keepdims=True) for m in mixed)
    scale = lax.rsqrt(ssq * (1.0 / C_C) + EPS)
    for gi, m in enumerate(mixed):
        cols = slice(gi * C_G, (gi + 1) * C_G)
        y = (m * scale * gc_ref[:, cols]).astype(yc_ref.dtype)
        for r in range(RUNS):
            yc_ref[r, :, cols] = y[r * tj:(r + 1) * tj]


def _local_mix(proj, conv_w, conv_b, ln_g, ln_b, pool_w, pool_scale, g_a, g_c):
    J = proj.shape[2]
    S = RUNS * J
    tj = min(MIX_TJ, J)
    hb = tj // HALO
    last_hb = J // HALO - 1
    conv_slabs, pool_slabs = 2 * C_A // LANES, C_C // LANES
    pool_blk = POOL_SLAB // pool_slabs
    cur_map = lambda i: (0, i, 0)
    const2 = lambda i: (0, 0)
    slab_specs = lambda n, first: [
        pl.BlockSpec((n, RUNS, tj, LANES), lambda i: (first, 0, i, 0)),
        pl.BlockSpec((n, RUNS, HALO, LANES), lambda i: (first, 0, jnp.maximum(i * hb - 1, 0), 0)),
        pl.BlockSpec((n, RUNS, HALO, LANES), lambda i: (first, 0, jnp.minimum((i + 1) * hb, last_hb), 0)),
    ]
    ya, yc = pl.pallas_call(
        functools.partial(_local_mix_kernel, seq_len=S),
        grid=(J // tj,),
        in_specs=slab_specs(conv_slabs, 0) + slab_specs(pool_slabs, pool_blk) + [
            pl.BlockSpec((CONV_WIDTH, C_A), const2),
            pl.BlockSpec((1, C_A), const2),
            pl.BlockSpec((1, C_A), const2),
            pl.BlockSpec((1, C_A), const2),
            pl.BlockSpec((len(POOL_WINDOWS), C_G, C_G), lambda i: (0, 0, 0)),
            pl.BlockSpec((1, C_C), const2),
            pl.BlockSpec((1, C_A), const2),
            pl.BlockSpec((1, C_C), const2),
        ],
        out_specs=[pl.BlockSpec((RUNS, tj, C_A), cur_map), pl.BlockSpec((RUNS, tj, C_C), cur_map)],
        out_shape=[jax.ShapeDtypeStruct((RUNS, J, C_A), BF16),
                   jax.ShapeDtypeStruct((RUNS, J, C_C), BF16)],
        scratch_shapes=[
            pltpu.VMEM((RUNS, tj + 2 * HALO, C_A), F32),
            pltpu.VMEM((RUNS, tj, C_A), F32),
            pltpu.VMEM((RUNS, tj, C_A), F32),
            pltpu.VMEM((RUNS, tj, C_A), F32),
            pltpu.VMEM((RUNS, tj + 2 * HALO, C_C), F32),
            pltpu.VMEM((RUNS, tj, C_C), F32),
            pltpu.VMEM((RUNS, tj, C_C), F32),
            pltpu.VMEM((RUNS * tj, C_C), F32),
        ],
        compiler_params=pltpu.CompilerParams(
            dimension_semantics=("arbitrary",), vmem_limit_bytes=VMEM_LIMIT),
        name="local_mix",
    )(proj, proj, proj, proj, proj, proj,
      conv_w, conv_b, ln_g, ln_b, pool_w, pool_scale, g_a, g_c)
    return ya.reshape(S, C_A), yc.reshape(S, C_C)


ATT_TQ = 128
ATT_W = ATT_TQ + 2 * HALF_WINDOW
ATT_UNROLL = 16
INTERIOR, FIRST, LAST = 0, 1, 2


def _block_layout(d, kind):
    pieces = RUNS // d
    q_rows, k_rows = ATT_TQ // pieces, ATT_W // pieces
    first_group = {INTERIOR: -HALF_WINDOW // pieces, FIRST: 0, LAST: (ATT_TQ - ATT_W) // pieces}[kind]
    a_q, u_q = np.divmod(np.arange(ATT_TQ), q_rows)
    a_k, u_k = np.divmod(np.arange(ATT_W), k_rows)
    return pieces * u_q + a_q, pieces * (u_k + first_group) + a_k, first_group


def _band_bias():
    tables = []
    for d in DILATIONS:
        for kind in (INTERIOR, FIRST, LAST):
            q_step, k_step, _ = _block_layout(d, kind)
            ok = np.abs(k_step[None, :] - q_step[:, None]) <= HALF_WINDOW
            tables.append(np.where(ok, 0.0, NEG))
    return np.stack(tables).astype(np.float32)


def _attention_kernel(q_ref, k_ref, v_ref, bias_ref, o_ref, m_acc, s_acc):
    S = o_ref.shape[0]
    J = S // RUNS
    lane = lax.broadcasted_iota(jnp.int32, (ATT_TQ, LANES), 1)
    head_a = lane < HEAD_DIM
    nt_dims = (((1,), (1,)), ((), ()))

    def gather(ref, starts, rows):
        return jnp.concatenate([ref[pl.ds(s, rows), :] for s in starts], axis=0)

    def attend(q, kw, vw, bias):
        q2, kw, vw = q.astype(BF16), kw.astype(BF16), vw.astype(BF16)
        zero = jnp.zeros_like(q2)

        def one_head(qh):
            s = lax.dot_general(qh, kw, nt_dims, preferred_element_type=F32) + bias
            m = jnp.max(s, axis=-1, keepdims=True)
            p = jnp.exp(s - m)
            l = jnp.sum(p, axis=-1, keepdims=True)
            o = jnp.dot(p.astype(BF16), vw, preferred_element_type=F32)
            return m, l, o

        ma, la, oa = one_head(jnp.where(head_a, q2, zero))
        mb, lb, ob = one_head(jnp.where(head_a, zero, q2))
        return jnp.where(head_a, ma, mb), jnp.where(head_a, la, lb), jnp.where(head_a, oa, ob)

    def block(pattern, kind, q_starts, k_starts, bias):
        pieces = len(q_starts)
        q_rows, k_rows = ATT_TQ // pieces, ATT_W // pieces
        q = gather(q_ref.at[0], q_starts, q_rows)
        kw = gather(k_ref.at[0], k_starts, k_rows)
        vw = gather(v_ref.at[0], k_starts, k_rows)
        m_blk, l_blk, o_blk = attend(q, kw, vw, bias)
        if pattern > 0:
            m_old = gather(m_acc, q_starts, q_rows)
            m_new = jnp.maximum(m_old, m_blk)
            w_old = jnp.exp(m_old - m_new)
            w_blk = jnp.exp(m_blk - m_new)
            l_blk = w_old * gather(s_acc, q_starts, q_rows) + w_blk * l_blk
            o_blk = w_old * gather(o_ref, q_starts, q_rows) + w_blk * o_blk
            m_blk = m_new
        for a, s in enumerate(q_starts):
            rows = slice(a * q_rows, (a + 1) * q_rows)
            m_acc[pl.ds(s, q_rows), :] = m_blk[rows]
            s_acc[pl.ds(s, q_rows), :] = l_blk[rows]
            o_ref[pl.ds(s, q_rows), :] = o_blk[rows]

    def aligned(x):
        if isinstance(x, int):
            assert x % SUBLANES == 0
            return x
        return pl.multiple_of(x, SUBLANES)

    for pattern, d in enumerate(DILATIONS):
        pieces = RUNS // d
        q_rows, k_rows = ATT_TQ // pieces, ATT_W // pieces
        n_blk = (S // d) // ATT_TQ
        assert n_blk >= 2, "first and last key blocks must be distinct"
        first_groups = [_block_layout(d, kind)[2] for kind in (INTERIOR, FIRST, LAST)]

        def run_block(sub, jb, kind, pattern=pattern, d=d, pieces=pieces, q_rows=q_rows,
                      first_groups=first_groups):
            g0 = jb * q_rows
            q_starts = [aligned((sub + d * a) * J + g0) for a in range(pieces)]
            if kind == INTERIOR and first_groups[kind] % SUBLANES:
                lead = SUBLANES - (-first_groups[kind]) % SUBLANES
                k_starts = [aligned((sub + d * a) * J + g0 - lead + first_groups[kind]) + lead
                            for a in range(pieces)]
            else:
                k_starts = [aligned((sub + d * a) * J + g0 + first_groups[kind]) for a in range(pieces)]
            block(pattern, kind, q_starts, k_starts, bias_ref[3 * pattern + kind])

        def kind_of(jb, n_blk=n_blk):
            return FIRST if jb == 0 else LAST if jb == n_blk - 1 else INTERIOR

        group = ATT_UNROLL
        if n_blk <= group:
            subs_per_trip = group // n_blk

            def short_subs(t, carry, run_block=run_block, n_blk=n_blk, subs_per_trip=subs_per_trip,
                           kind_of=kind_of):
                for k in range(subs_per_trip):
                    for jb in range(n_blk):
                        run_block(t * subs_per_trip + k, jb, kind_of(jb))
                return carry

            assert d % subs_per_trip == 0
            lax.fori_loop(0, d // subs_per_trip, short_subs, 0)
        else:
            n_mid = n_blk - 2 * group
            assert n_mid >= 0 and n_mid % group == 0

            def long_sub(sub, carry, run_block=run_block, n_blk=n_blk, n_mid=n_mid, kind_of=kind_of,
                         group=group):
                for jb in range(group):
                    run_block(sub, jb, kind_of(jb))

                def middle(t, c):
                    for k in range(group):
                        run_block(sub, group + t * group + k, INTERIOR)
                    return c

                if n_mid:
                    lax.fori_loop(0, n_mid // group, middle, 0)
                for jb in range(n_blk - group, n_blk):
                    run_block(sub, jb, kind_of(jb))
                return carry

            if d == 1:
                long_sub(0, 0)
            else:
                lax.fori_loop(0, d, long_sub, 0)

    def normalise(c, carry):
        rows = pl.ds(pl.multiple_of(c * ATT_W, ATT_W), ATT_W)
        o_ref[rows, :] = o_ref[rows, :] / s_acc[rows, :]
        return carry

    lax.fori_loop(0, S // ATT_W, normalise, 0)


def _attention(proj, bias):
    S = proj.shape[1] * proj.shape[2]
    qkv = proj.reshape(N_SLABS, S, LANES)
    qkv_spec = lambda off: pl.BlockSpec((1, S, LANES), lambda p: (p + off, 0, 0))
    return pl.pallas_call(
        _attention_kernel,
        grid=(N_PAIRS,),
        in_specs=[
            qkv_spec(Q_SLAB), qkv_spec(K_SLAB), qkv_spec(V_SLAB),
            pl.BlockSpec(bias.shape, lambda p: (0, 0, 0)),
        ],
        out_specs=pl.BlockSpec((S, LANES), lambda p: (0, p)),
        out_shape=jax.ShapeDtypeStruct((S, C_B), F32),
        scratch_shapes=[pltpu.VMEM((S, LANES), F32), pltpu.VMEM((S, LANES), F32)],
        compiler_params=pltpu.CompilerParams(
            dimension_semantics=("arbitrary",), vmem_limit_bytes=VMEM_LIMIT),
        name="attention",
    )(qkv, qkv, qkv, bias)


def _out_proj_kernel(ya_ref, yb_ref, yc_ref, gb_ref, w_ref, x_ref, gp_ref, o_ref, *by_run,
                     x_is_natural, out_is_natural):
    yb = _load_rows(yb_ref)
    ybn = (yb * _rms_scale(yb) * gb_ref[...]).astype(BF16)
    acc = jnp.dot(_load_rows(ya_ref), w_ref[0:C_A, :], preferred_element_type=F32)
    acc = acc + jnp.dot(ybn, w_ref[C_A:C_A + C_B, :], preferred_element_type=F32)
    acc = acc + jnp.dot(_load_rows(yc_ref), w_ref[C_A + C_B:, :], preferred_element_type=F32)
    update = acc * _rms_scale(acc) * gp_ref[...]
    if x_is_natural:
        rows = update.shape[0] // RUNS
        _regroup_by_run(x_ref, by_run[0])
        for r in range(RUNS):
            o_ref[r] = by_run[0][r, 0:rows, :] + update[r * rows:(r + 1) * rows]
    elif out_is_natural:
        y = _load_rows(x_ref) + update
        rows = y.shape[0] // RUNS
        for r in range(RUNS):
            o_ref[:, r, :] = y[r * rows:(r + 1) * rows]
    else:
        o_ref[...] = x_ref[...] + update


def _out_proj(ya, yb, yc, g_b, w, layer, x, x_is_natural, out_is_natural, g_post):
    S = ya.shape[0]
    tm = _row_tile(S)
    const2 = lambda i: (0, 0)
    assert not (x_is_natural and out_is_natural)
    mixed = x_is_natural or out_is_natural
    ya_view, ya_spec = _rows_spec(S, C_A, mixed=mixed)
    yb_view, yb_spec = _rows_spec(S, C_B, mixed=mixed)
    yc_view, yc_spec = _rows_spec(S, C_C, mixed=mixed)
    res_view, res_spec = _rows_spec(S, D_MODEL, mixed=mixed)
    x_view, x_spec = _natural_spec(S, D_MODEL) if x_is_natural else (res_view, res_spec)
    out_view, out_spec = _natural_spec(S, D_MODEL) if out_is_natural else (res_view, res_spec)
    out = pl.pallas_call(
        functools.partial(_out_proj_kernel, x_is_natural=x_is_natural, out_is_natural=out_is_natural),
        grid=(S // tm,),
        in_specs=[
            ya_spec, yb_spec, yc_spec,
            pl.BlockSpec((1, C_B), const2),
            pl.BlockSpec((None, MIX_WIDTH, D_MODEL), lambda i: (layer, 0, 0)),
            x_spec,
            pl.BlockSpec((1, D_MODEL), const2),
        ],
        out_specs=out_spec,
        out_shape=jax.ShapeDtypeStruct(out_view, F32),
        scratch_shapes=(
            [pltpu.VMEM((RUNS, _regroup_pitch(tm // RUNS), D_MODEL), F32)] if x_is_natural else []),
        compiler_params=pltpu.CompilerParams(
            dimension_semantics=("arbitrary",), vmem_limit_bytes=VMEM_LIMIT),
        name="out_proj",
    )(ya.reshape(ya_view), yb.reshape(yb_view), yc.reshape(yc_view), g_b, w, x.reshape(x_view), g_post)
    return out.reshape(S, D_MODEL)


FFN_TM = 1024
FFN_TH = 256
FFN_VMEM_LIMIT = 58 * 1024 * 1024


def _ffn_kernel(x_ref, gpre_ref, wg_ref, wu_ref, wd_ref, gpost_ref, o_ref, h_ref):
    j = pl.program_id(1)

    @pl.when(j == 0)
    def _():
        x = x_ref[...]
        h_ref[...] = (x * _rms_scale(x) * gpre_ref[...]).astype(BF16)
        o_ref[...] = jnp.zeros_like(o_ref)

    h = h_ref[...]
    gate = jnp.dot(h, wg_ref[...].astype(BF16), preferred_element_type=F32)
    up = jnp.dot(h, wu_ref[...].astype(BF16), preferred_element_type=F32)
    act = (gate * jax.nn.sigmoid(gate) * up).astype(BF16)
    o_ref[...] += jnp.dot(act, wd_ref[...].astype(BF16), preferred_element_type=F32)

    @pl.when(j == pl.num_programs(1) - 1)
    def _():
        f = o_ref[...]
        o_ref[...] = x_ref[...] + f * _rms_scale(f) * gpost_ref[...]


def _ffn(x, g_pre, w_gate, w_up, w_down, layer, g_post):
    S = x.shape[0]
    tm = min(FFN_TM, S // 2)
    row_spec = pl.BlockSpec((tm, D_MODEL), lambda i, j: (i, 0))
    return pl.pallas_call(
        _ffn_kernel,
        grid=(S // tm, FFN_HIDDEN // FFN_TH),
        in_specs=[
            row_spec,
            pl.BlockSpec((1, D_MODEL), lambda i, j: (0, 0)),
            pl.BlockSpec((None, D_MODEL, FFN_TH), lambda i, j: (layer, 0, j)),
            pl.BlockSpec((None, D_MODEL, FFN_TH), lambda i, j: (layer, 0, j)),
            pl.BlockSpec((None, FFN_TH, D_MODEL), lambda i, j: (layer, j, 0)),
            pl.BlockSpec((1, D_MODEL), lambda i, j: (0, 0)),
        ],
        out_specs=row_spec,
        out_shape=jax.ShapeDtypeStruct((S, D_MODEL), F32),
        scratch_shapes=[pltpu.VMEM((tm, D_MODEL), BF16)],
        compiler_params=pltpu.CompilerParams(
            dimension_semantics=("arbitrary", "arbitrary"), vmem_limit_bytes=FFN_VMEM_LIMIT),
        name="ffn",
    )(x, g_pre, w_gate, w_up, w_down, g_post)


def _rope_lane_tables(S):
    half = ROT_DIM // 2
    J = S // RUNS
    pos = (np.arange(J)[None, :] * RUNS + np.arange(RUNS)[:, None]).reshape(S).astype(np.float64)
    inv = ROPE_THETA ** (-np.arange(0, ROT_DIM, 2, dtype=np.float64) / ROT_DIM)
    ang = pos[:, None] * inv[None, :]
    cos, sin = np.cos(ang), np.sin(ang)
    ones = np.ones((S, HEAD_DIM - ROT_DIM))
    zeros_h = np.zeros((S, half))
    zeros_r = np.zeros((S, HEAD_DIM - ROT_DIM))
    cos_h = np.concatenate([cos, cos, ones], axis=1)
    sin_lo_h = np.concatenate([-sin, zeros_h, zeros_r], axis=1)
    sin_hi_h = np.concatenate([zeros_h, sin, zeros_r], axis=1)
    pair = lambda t: jnp.asarray(np.concatenate([t, t], axis=1).astype(np.float32))
    return pair(cos_h), pair(sin_lo_h), pair(sin_hi_h)


def kernel(x, w_in, conv_w, conv_b, conv_ln_g, conv_ln_b, pool_w, pool_scale, g_mix, w_out,
           g_pre_mix, g_post_mix, g_pre_ffn, g_post_ffn, w_gate, w_up, w_down):
    B, S, _ = x.shape
    depth = w_in.shape[0]
    assert B == 1, "the sequence-mixing kernels treat all rows as one sequence"
    cos_t, sin_lo_t, sin_hi_t = _rope_lane_tables(S)
    bias = jnp.asarray(_band_bias())
    row = lambda v: v.reshape(1, -1)
    w_in, w_out, pool_w = (t.astype(BF16) for t in (w_in, w_out, pool_w))

    xb = x.reshape(S, D_MODEL)
    for l in range(depth):
        natural = l == 0
        proj = _in_proj(xb, natural, row(g_pre_mix[l]), w_in, l, cos_t, sin_lo_t, sin_hi_t)
        ya, yc = _local_mix(proj, conv_w[l], row(conv_b[l]), row(conv_ln_g[l]),
                            row(conv_ln_b[l]), pool_w[l], row(pool_scale[l]),
                            row(g_mix[l, :C_A]), row(g_mix[l, C_A + C_B:]))
        yb = _attention(proj, bias)
        xb = _out_proj(ya, yb, yc, row(g_mix[l, C_A:C_A + C_B]), w_out, l, xb, natural, l == depth - 1,
                       row(g_post_mix[l]))
        xb = _ffn(xb, row(g_pre_ffn[l]), w_gate, w_up, w_down, l, row(g_post_ffn[l]))
    return xb.reshape(B, S, D_MODEL)
```

```python
import functools

import numpy as np
import jax
import jax.numpy as jnp
from jax import lax
from jax.experimental import pallas as pl
from jax.experimental.pallas import tpu as pltpu

F32 = jnp.float32
BF16 = jnp.bfloat16

D_MODEL = 2048
C_A = 512
N_HEADS = 16
HEAD_DIM = 64
C_B = N_HEADS * HEAD_DIM
C_C = 512
POOL_WINDOWS = (2, 4, 8, 16)
C_G = C_C // len(POOL_WINDOWS)
MIX_WIDTH = C_A + C_B + C_C
IN_WIDTH = 2 * C_A + 3 * C_B + C_C
CONV_WIDTH = 31
CONV_HALF = CONV_WIDTH // 2
DILATIONS = (1, 4, 16)
HALF_WINDOW = 64
ROT_DIM = HEAD_DIM // 4
ROPE_THETA = 500000.0
FFN_HIDDEN = 5632
EPS = 1e-6
NEG = -1e30

LANES = 128
SUBLANES = 8
N_PAIRS = N_HEADS // 2
RUNS = DILATIONS[-1]
ROW_TILE = 512
VMEM_LIMIT = 52 * 1024 * 1024


def _rms_scale(t):
    return lax.rsqrt(jnp.mean(t * t, axis=-1, keepdims=True) + EPS)


def _row_tile(S):
    tm = min(ROW_TILE, S // RUNS)
    assert (S // RUNS) % tm == 0 and tm % (RUNS * 2 * SUBLANES) == 0
    return tm


def _rows_spec(S, width, mixed, tm=None):
    J, tm = S // RUNS, tm or _row_tile(S)
    if mixed:
        return (RUNS, J, width), pl.BlockSpec((RUNS, tm // RUNS, width), lambda i, *_: (0, i, 0))
    return (S, width), pl.BlockSpec((tm, width), lambda i, *_: (i, 0))


def _natural_spec(S, width):
    J, tm = S // RUNS, _row_tile(S)
    return (J, RUNS, width), pl.BlockSpec((tm // RUNS, RUNS, width), lambda i, *_: (i, 0, 0))


def _load_rows(ref):
    v = ref[...]
    return v.reshape(-1, v.shape[-1]) if v.ndim == 3 else v


def _regroup_pitch(rows):
    return rows + SUBLANES if (rows // SUBLANES) % 2 == 0 else rows


def _regroup_by_run(x_ref, by_run_ref):
    for j in range(x_ref.shape[0]):
        by_run_ref[:, j, :] = x_ref[j]


IN_TN = 1536
IN_CHUNK = 256
N_SLABS = IN_WIDTH // LANES
Q_SLAB, K_SLAB, V_SLAB, POOL_SLAB = 8, 16, 24, 32
IN_RUNS_PER_TILE = 2


def _in_proj_kernel(x_ref, g_ref, w_ref, cos_ref, sin_lo_ref, sin_hi_ref, proj_ref, h_ref,
                    *by_run, x_is_natural):
    j = pl.program_id(1)
    tm = h_ref.shape[0]

    @pl.when(j == 0)
    def _():
        if x_is_natural:
            rows = tm // RUNS
            _regroup_by_run(x_ref, by_run[0])
            for r in range(RUNS):
                x = by_run[0][r, 0:rows, :]
                h_ref[r * rows:(r + 1) * rows, :] = (x * _rms_scale(x) * g_ref[...]).astype(BF16)
        else:
            x = x_ref[...]
            h_ref[...] = (x * _rms_scale(x) * g_ref[...]).astype(BF16)

    slabs_per_step = IN_TN // LANES
    for step in range(IN_WIDTH // IN_TN):
        @pl.when(j == step)
        def _(step=step):
            for c in range(IN_TN // IN_CHUNK):
                acc = jnp.dot(h_ref[...], w_ref[:, c * IN_CHUNK:(c + 1) * IN_CHUNK],
                              preferred_element_type=F32)
                for t in range(IN_CHUNK // LANES):
                    local = c * (IN_CHUNK // LANES) + t
                    slab = step * slabs_per_step + local
                    blk = acc[:, t * LANES:(t + 1) * LANES]
                    if Q_SLAB <= slab < V_SLAB:
                        blk = (blk * _load_rows(cos_ref)
                               + pltpu.roll(blk, ROT_DIM // 2, 1) * _load_rows(sin_hi_ref)
                               + pltpu.roll(blk, LANES - ROT_DIM // 2, 1) * _load_rows(sin_lo_ref))
                        if slab < K_SLAB:
                            blk = blk * HEAD_DIM ** -0.5
                    proj_ref[local] = blk.reshape(proj_ref.shape[1:])


def _in_proj(x, x_is_natural, g, w, layer, cos_t, sin_lo_t, sin_hi_t):
    S = x.shape[0]
    J = S // RUNS
    slabs_per_step = IN_TN // LANES
    if x_is_natural:
        tm = _row_tile(S)
        view, x_spec = _natural_spec(S, D_MODEL)
        tab_view, tab_spec = _rows_spec(S, LANES, mixed=True)
        out_spec = pl.BlockSpec((slabs_per_step, RUNS, tm // RUNS, LANES), lambda i, j: (j, 0, i, 0))
    else:
        tm = IN_RUNS_PER_TILE * J
        view, x_spec = _rows_spec(S, D_MODEL, mixed=False, tm=tm)
        tab_view, tab_spec = _rows_spec(S, LANES, mixed=False, tm=tm)
        out_spec = pl.BlockSpec((slabs_per_step, IN_RUNS_PER_TILE, J, LANES), lambda i, j: (j, i, 0, 0))
    tables = [t.reshape(tab_view) for t in (cos_t, sin_lo_t, sin_hi_t)]
    return pl.pallas_call(
        functools.partial(_in_proj_kernel, x_is_natural=x_is_natural),
        grid=(S // tm, IN_WIDTH // IN_TN),
        in_specs=[
            x_spec,
            pl.BlockSpec((1, D_MODEL), lambda i, j: (0, 0)),
            pl.BlockSpec((None, D_MODEL, IN_TN), lambda i, j: (layer, 0, j)),
            tab_spec, tab_spec, tab_spec,
        ],
        out_specs=out_spec,
        out_shape=jax.ShapeDtypeStruct((N_SLABS, RUNS, J, LANES), F32),
        scratch_shapes=[pltpu.VMEM((tm, D_MODEL), BF16)] + (
            [pltpu.VMEM((RUNS, _regroup_pitch(tm // RUNS), D_MODEL), F32)] if x_is_natural else []),
        compiler_params=pltpu.CompilerParams(
            dimension_semantics=("arbitrary", "arbitrary"), vmem_limit_bytes=VMEM_LIMIT),
        name="in_proj",
    )(x.reshape(view), g, w, *tables)


MIX_TJ = 32
HALO = SUBLANES


def _tap_source(r, offset):
    q = r + offset
    return q % RUNS, q // RUNS


def _local_mix_kernel(cu_ref, cu_prev_ref, cu_next_ref, pu_ref, pu_prev_ref, pu_next_ref,
                      cw_ref, cb_ref, lng_ref, lnb_ref, pw_ref, ps_ref, ga_ref, gc_ref,
                      ya_ref, yc_ref, hbuf, h_dn, h_up, cbuf, ubuf, u_dn, u_up, pbuf, *, seq_len):
    i = pl.program_id(0)
    n = pl.num_programs(0)
    tj = ya_ref.shape[1]
    not_first = (i > 0).astype(F32)
    not_last = (i < n - 1).astype(F32)
    mid = slice(HALO, HALO + tj)

    def glu(ref, lc, r):
        return ref[lc, r] * jax.nn.sigmoid(ref[C_A // LANES + lc, r])

    n_lc = C_A // LANES
    for r in range(RUNS):
        for lc in range(n_lc):
            cols = slice(lc * LANES, (lc + 1) * LANES)
            hbuf[r, 0:HALO, cols] = glu(cu_prev_ref, lc, r) * not_first
            hbuf[r, mid, cols] = glu(cu_ref, lc, r)
            hbuf[r, HALO + tj:, cols] = glu(cu_next_ref, lc, r) * not_last
        for gi in range(len(POOL_WINDOWS)):
            cols = slice(gi * C_G, (gi + 1) * C_G)
            ubuf[r, 0:HALO, cols] = pu_prev_ref[gi, r] * not_first
            ubuf[r, mid, cols] = pu_ref[gi, r]
            ubuf[r, HALO + tj:, cols] = pu_next_ref[gi, r] * not_last
    for r in range(RUNS):
        h_dn[r] = hbuf[r, HALO - 1:HALO - 1 + tj, :]
        h_up[r] = hbuf[r, HALO + 1:HALO + 1 + tj, :]
        u_dn[r] = ubuf[r, HALO - 1:HALO - 1 + tj, :]
        u_up[r] = ubuf[r, HALO + 1:HALO + 1 + tj, :]

    def shifted(center, dn, up, run, shift, cols):
        if shift == 0:
            return center[run, mid, cols]
        return (dn if shift < 0 else up)[run, :, cols]

    for r in range(RUNS):
        for lc in range(C_A // LANES):
            cols = slice(lc * LANES, (lc + 1) * LANES)
            acc = jnp.zeros((tj, LANES), F32)
            for t in range(CONV_WIDTH):
                run, shift = _tap_source(r, t - CONV_HALF)
                acc = acc + shifted(hbuf, h_dn, h_up, run, shift, cols) * cw_ref[t:t + 1, cols]
            cbuf[r, :, cols] = acc + cb_ref[:, cols]

    for r in range(RUNS):
        c = cbuf[r]
        mu = jnp.mean(c, axis=-1, keepdims=True)
        cc = c - mu
        var = jnp.mean(cc * cc, axis=-1, keepdims=True)
        y = cc * lax.rsqrt(var + EPS) * lng_ref[...] + lnb_ref[...]
        y = y * jax.nn.sigmoid(y)
        ya_ref[r] = (y * _rms_scale(y) * ga_ref[...]).astype(ya_ref.dtype)

    row = lax.broadcasted_iota(jnp.int32, (tj, LANES), 0)
    for r in range(RUNS):
        pos = (i * tj + row) * RUNS + r
        for gi, win in enumerate(POOL_WINDOWS):
            cols = slice(gi * C_G, (gi + 1) * C_G)
            half = win // 2
            tot = jnp.zeros((tj, LANES), F32)
            for o in range(-half, win - half):
                run, shift = _tap_source(r, o)
                tot = tot + shifted(ubuf, u_dn, u_up, run, shift, cols)
            lo = jnp.maximum(pos - half, 0)
            hi = jnp.minimum(pos + (win - half), seq_len)
            pbuf[r * tj:(r + 1) * tj, cols] = tot / (hi - lo).astype(F32) - ubuf[r, mid, cols]

    mixed = []
    for gi in range(len(POOL_WINDOWS)):
        cols = slice(gi * C_G, (gi + 1) * C_G)
        m = jnp.dot(pbuf[:, cols].astype(BF16), pw_ref[gi], preferred_element_type=F32)
        mixed.append(m * ps_ref[:, cols])
    ssq = sum(jnp.sum(m * m, axis=-1, keepdims=True) for m in mixed)
    scale = lax.rsqrt(ssq * (1.0 / C_C) + EPS)
    for gi, m in enumerate(mixed):
        cols = slice(gi * C_G, (gi + 1) * C_G)
        y = (m * scale * gc_ref[:, cols]).astype(yc_ref.dtype)
        for r in range(RUNS):
            yc_ref[r, :, cols] = y[r * tj:(r + 1) * tj]


def _local_mix(proj, conv_w, conv_b, ln_g, ln_b, pool_w, pool_scale, g_a, g_c):
    J = proj.shape[2]
    S = RUNS * J
    tj = min(MIX_TJ, J)
    hb = tj // HALO
    last_hb = J // HALO - 1
    conv_slabs, pool_slabs = 2 * C_A // LANES, C_C // LANES
    pool_blk = POOL_SLAB // pool_slabs
    cur_map = lambda i: (0, i, 0)
    const2 = lambda i: (0, 0)
    slab_specs = lambda n, first: [
        pl.BlockSpec((n, RUNS, tj, LANES), lambda i: (first, 0, i, 0)),
        pl.BlockSpec((n, RUNS, HALO, LANES), lambda i: (first, 0, jnp.maximum(i * hb - 1, 0), 0)),
        pl.BlockSpec((n, RUNS, HALO, LANES), lambda i: (first, 0, jnp.minimum((i + 1) * hb, last_hb), 0)),
    ]
    ya, yc = pl.pallas_call(
        functools.partial(_local_mix_kernel, seq_len=S),
        grid=(J // tj,),
        in_specs=slab_specs(conv_slabs, 0) + slab_specs(pool_slabs, pool_blk) + [
            pl.BlockSpec((CONV_WIDTH, C_A), const2),
            pl.BlockSpec((1, C_A), const2),
            pl.BlockSpec((1, C_A), const2),
            pl.BlockSpec((1, C_A), const2),
            pl.BlockSpec((len(POOL_WINDOWS), C_G, C_G), lambda i: (0, 0, 0)),
            pl.BlockSpec((1, C_C), const2),
            pl.BlockSpec((1, C_A), const2),
            pl.BlockSpec((1, C_C), const2),
        ],
        out_specs=[pl.BlockSpec((RUNS, tj, C_A), cur_map), pl.BlockSpec((RUNS, tj, C_C), cur_map)],
        out_shape=[jax.ShapeDtypeStruct((RUNS, J, C_A), BF16),
                   jax.ShapeDtypeStruct((RUNS, J, C_C), BF16)],
        scratch_shapes=[
            pltpu.VMEM((RUNS, tj + 2 * HALO, C_A), F32),
            pltpu.VMEM((RUNS, tj, C_A), F32),
            pltpu.VMEM((RUNS, tj, C_A), F32),
            pltpu.VMEM((RUNS, tj, C_A), F32),
            pltpu.VMEM((RUNS, tj + 2 * HALO, C_C), F32),
            pltpu.VMEM((RUNS, tj, C_C), F32),
            pltpu.VMEM((RUNS, tj, C_C), F32),
            pltpu.VMEM((RUNS * tj, C_C), F32),
        ],
        compiler_params=pltpu.CompilerParams(
            dimension_semantics=("arbitrary",), vmem_limit_bytes=VMEM_LIMIT),
        name="local_mix",
    )(proj, proj, proj, proj, proj, proj,
      conv_w, conv_b, ln_g, ln_b, pool_w, pool_scale, g_a, g_c)
    return ya.reshape(S, C_A), yc.reshape(S, C_C)


ATT_TQ = 128
ATT_W = ATT_TQ + 2 * HALF_WINDOW
ATT_UNROLL = 16
INTERIOR, FIRST, LAST = 0, 1, 2


def _block_layout(d, kind):
    pieces = RUNS // d
    q_rows, k_rows = ATT_TQ // pieces, ATT_W // pieces
    first_group = {INTERIOR: -HALF_WINDOW // pieces, FIRST: 0, LAST: (ATT_TQ - ATT_W) // pieces}[kind]
    a_q, u_q = np.divmod(np.arange(ATT_TQ), q_rows)
    a_k, u_k = np.divmod(np.arange(ATT_W), k_rows)
    return pieces * u_q + a_q, pieces * (u_k + first_group) + a_k, first_group


def _band_bias():
    tables = []
    for d in DILATIONS:
        for kind in (INTERIOR, FIRST, LAST):
            q_step, k_step, _ = _block_layout(d, kind)
            ok = np.abs(k_step[None, :] - q_step[:, None]) <= HALF_WINDOW
            tables.append(np.where(ok, 0.0, NEG))
    return np.stack(tables).astype(np.float32)


def _attention_kernel(q_ref, k_ref, v_ref, bias_ref, o_ref, m_acc, s_acc):
    S = o_ref.shape[0]
    J = S // RUNS
    lane = lax.broadcasted_iota(jnp.int32, (ATT_TQ, LANES), 1)
    head_a = lane < HEAD_DIM
    nt_dims = (((1,), (1,)), ((), ()))

    def gather(ref, starts, rows):
        return jnp.concatenate([ref[pl.ds(s, rows), :] for s in starts], axis=0)

    def attend(q, kw, vw, bias):
        q2, kw, vw = q.astype(BF16), kw.astype(BF16), vw.astype(BF16)
        zero = jnp.zeros_like(q2)

        def one_head(qh):
            s = lax.dot_general(qh, kw, nt_dims, preferred_element_type=F32) + bias
            m = jnp.max(s, axis=-1, keepdims=True)
            p = jnp.exp(s - m)
            l = jnp.sum(p, axis=-1, keepdims=True)
            o = jnp.dot(p.astype(BF16), vw, preferred_element_type=F32)
            return m, l, o

        ma, la, oa = one_head(jnp.where(head_a, q2, zero))
        mb, lb, ob = one_head(jnp.where(head_a, zero, q2))
        return jnp.where(head_a, ma, mb), jnp.where(head_a, la, lb), jnp.where(head_a, oa, ob)

    def block(pattern, kind, q_starts, k_starts, bias):
        pieces = len(q_starts)
        q_rows, k_rows = ATT_TQ // pieces, ATT_W // pieces
        q = gather(q_ref.at[0], q_starts, q_rows)
        kw = gather(k_ref.at[0], k_starts, k_rows)
        vw = gather(v_ref.at[0], k_starts, k_rows)
        m_blk, l_blk, o_blk = attend(q, kw, vw, bias)
        if pattern > 0:
            m_old = gather(m_acc, q_starts, q_rows)
            m_new = jnp.maximum(m_old, m_blk)
            w_old = jnp.exp(m_old - m_new)
            w_blk = jnp.exp(m_blk - m_new)
            l_blk = w_old * gather(s_acc, q_starts, q_rows) + w_blk * l_blk
            o_blk = w_old * gather(o_ref, q_starts, q_rows) + w_blk * o_blk
            m_blk = m_new
        for a, s in enumerate(q_starts):
            rows = slice(a * q_rows, (a + 1) * q_rows)
            m_acc[pl.ds(s, q_rows), :] = m_blk[rows]
            s_acc[pl.ds(s, q_rows), :] = l_blk[rows]
            o_ref[pl.ds(s, q_rows), :] = o_blk[rows]

    def aligned(x):
        if isinstance(x, int):
            assert x % SUBLANES == 0
            return x
        return pl.multiple_of(x, SUBLANES)

    for pattern, d in enumerate(DILATIONS):
        pieces = RUNS // d
        q_rows, k_rows = ATT_TQ // pieces, ATT_W // pieces
        n_blk = (S // d) // ATT_TQ
        assert n_blk >= 2, "first and last key blocks must be distinct"
        first_groups = [_block_layout(d, kind)[2] for kind in (INTERIOR, FIRST, LAST)]

        def run_block(sub, jb, kind, pattern=pattern, d=d, pieces=pieces, q_rows=q_rows,
                      first_groups=first_groups):
            g0 = jb * q_rows
            q_starts = [aligned((sub + d * a) * J + g0) for a in range(pieces)]
            if kind == INTERIOR and first_groups[kind] % SUBLANES:
                lead = SUBLANES - (-first_groups[kind]) % SUBLANES
                k_starts = [aligned((sub + d * a) * J + g0 - lead + first_groups[kind]) + lead
                            for a in range(pieces)]
            else:
                k_starts = [aligned((sub + d * a) * J + g0 + first_groups[kind]) for a in range(pieces)]
            block(pattern, kind, q_starts, k_starts, bias_ref[3 * pattern + kind])

        def kind_of(jb, n_blk=n_blk):
            return FIRST if jb == 0 else LAST if jb == n_blk - 1 else INTERIOR

        group = ATT_UNROLL
        if n_blk <= group:
            subs_per_trip = group // n_blk

            def short_subs(t, carry, run_block=run_block, n_blk=n_blk, subs_per_trip=subs_per_trip,
                           kind_of=kind_of):
                for k in range(subs_per_trip):
                    for jb in range(n_blk):
                        run_block(t * subs_per_trip + k, jb, kind_of(jb))
                return carry

            assert d % subs_per_trip == 0
            lax.fori_loop(0, d // subs_per_trip, short_subs, 0)
        else:
            n_mid = n_blk - 2 * group
            assert n_mid >= 0 and n_mid % group == 0

            def long_sub(sub, carry, run_block=run_block, n_blk=n_blk, n_mid=n_mid, kind_of=kind_of,
                         group=group):
                for jb in range(group):
                    run_block(sub, jb, kind_of(jb))

                def middle(t, c):
                    for k in range(group):
                        run_block(sub, group + t * group + k, INTERIOR)
                    return c

                if n_mid:
                    lax.fori_loop(0, n_mid // group, middle, 0)
                for jb in range(n_blk - group, n_blk):
                    run_block(sub, jb, kind_of(jb))
                return carry

            if d == 1:
                long_sub(0, 0)
            else:
                lax.fori_loop(0, d, long_sub, 0)

    def normalise(c, carry):
        rows = pl.ds(pl.multiple_of(c * ATT_W, ATT_W), ATT_W)
        o_ref[rows, :] = o_ref[rows, :] / s_acc[rows, :]
        return carry

    lax.fori_loop(0, S // ATT_W, normalise, 0)


def _attention(proj, bias):
    S = proj.shape[1] * proj.shape[2]
    qkv = proj.reshape(N_SLABS, S, LANES)
    qkv_spec = lambda off: pl.BlockSpec((1, S, LANES), lambda p: (p + off, 0, 0))
    return pl.pallas_call(
        _attention_kernel,
        grid=(N_PAIRS,),
        in_specs=[
            qkv_spec(Q_SLAB), qkv_spec(K_SLAB), qkv_spec(V_SLAB),
            pl.BlockSpec(bias.shape, lambda p: (0, 0, 0)),
        ],
        out_specs=pl.BlockSpec((S, LANES), lambda p: (0, p)),
        out_shape=jax.ShapeDtypeStruct((S, C_B), F32),
        scratch_shapes=[pltpu.VMEM((S, LANES), F32), pltpu.VMEM((S, LANES), F32)],
        compiler_params=pltpu.CompilerParams(
            dimension_semantics=("arbitrary",), vmem_limit_bytes=VMEM_LIMIT),
        name="attention",
    )(qkv, qkv, qkv, bias)


OUT_CAST_ROWS = 256
OUT_VMEM_LIMIT = 58 * 1024 * 1024


def _out_proj_kernel(ya_ref, yb_ref, yc_ref, gb_ref, w_ref, x_ref, gp_ref, o_ref, wb_ref, *by_run,
                     x_is_natural, out_is_natural):
    @pl.when(pl.program_id(0) == 0)
    def _():
        for c in range(MIX_WIDTH // OUT_CAST_ROWS):
            rows = slice(c * OUT_CAST_ROWS, (c + 1) * OUT_CAST_ROWS)
            wb_ref[rows, :] = w_ref[rows, :].astype(BF16)

    yb = _load_rows(yb_ref)
    ybn = (yb * _rms_scale(yb) * gb_ref[...]).astype(BF16)
    acc = jnp.dot(_load_rows(ya_ref), wb_ref[0:C_A, :], preferred_element_type=F32)
    acc = acc + jnp.dot(ybn, wb_ref[C_A:C_A + C_B, :], preferred_element_type=F32)
    acc = acc + jnp.dot(_load_rows(yc_ref), wb_ref[C_A + C_B:, :], preferred_element_type=F32)
    update = acc * _rms_scale(acc) * gp_ref[...]
    if x_is_natural:
        rows = update.shape[0] // RUNS
        _regroup_by_run(x_ref, by_run[0])
        for r in range(RUNS):
            o_ref[r] = by_run[0][r, 0:rows, :] + update[r * rows:(r + 1) * rows]
    elif out_is_natural:
        y = _load_rows(x_ref) + update
        rows = y.shape[0] // RUNS
        for r in range(RUNS):
            o_ref[:, r, :] = y[r * rows:(r + 1) * rows]
    else:
        o_ref[...] = x_ref[...] + update


def _out_proj(ya, yb, yc, g_b, w, layer, x, x_is_natural, out_is_natural, g_post):
    S = ya.shape[0]
    tm = _row_tile(S)
    const2 = lambda i: (0, 0)
    assert not (x_is_natural and out_is_natural)
    mixed = x_is_natural or out_is_natural
    ya_view, ya_spec = _rows_spec(S, C_A, mixed=mixed)
    yb_view, yb_spec = _rows_spec(S, C_B, mixed=mixed)
    yc_view, yc_spec = _rows_spec(S, C_C, mixed=mixed)
    res_view, res_spec = _rows_spec(S, D_MODEL, mixed=mixed)
    x_view, x_spec = _natural_spec(S, D_MODEL) if x_is_natural else (res_view, res_spec)
    out_view, out_spec = _natural_spec(S, D_MODEL) if out_is_natural else (res_view, res_spec)
    out = pl.pallas_call(
        functools.partial(_out_proj_kernel, x_is_natural=x_is_natural, out_is_natural=out_is_natural),
        grid=(S // tm,),
        in_specs=[
            ya_spec, yb_spec, yc_spec,
            pl.BlockSpec((1, C_B), const2),
            pl.BlockSpec((None, MIX_WIDTH, D_MODEL), lambda i: (layer, 0, 0), pipeline_mode=pl.Buffered(1)),
            x_spec,
            pl.BlockSpec((1, D_MODEL), const2),
        ],
        out_specs=out_spec,
        out_shape=jax.ShapeDtypeStruct(out_view, F32),
        scratch_shapes=[pltpu.VMEM((MIX_WIDTH, D_MODEL), BF16)] + (
            [pltpu.VMEM((RUNS, _regroup_pitch(tm // RUNS), D_MODEL), F32)] if x_is_natural else []),
        compiler_params=pltpu.CompilerParams(
            dimension_semantics=("arbitrary",), vmem_limit_bytes=OUT_VMEM_LIMIT),
        name="out_proj",
    )(ya.reshape(ya_view), yb.reshape(yb_view), yc.reshape(yc_view), g_b, w, x.reshape(x_view), g_post)
    return out.reshape(S, D_MODEL)


FFN_TM = 1024
FFN_TH = 256
FFN_VMEM_LIMIT = 58 * 1024 * 1024


def _ffn_kernel(x_ref, gpre_ref, wg_ref, wu_ref, wd_ref, gpost_ref, o_ref, h_ref, act_ref):
    j = pl.program_id(1)
    last = pl.num_programs(1) - 1

    @pl.when(j == 0)
    def _():
        x = x_ref[...]
        h_ref[...] = (x * _rms_scale(x) * gpre_ref[...]).astype(BF16)
        o_ref[...] = jnp.zeros_like(o_ref)
        act_ref[...] = jnp.zeros_like(act_ref)

    def down(act):
        return jnp.dot(act, wd_ref[...].astype(BF16), preferred_element_type=F32)

    @pl.when(j < last)
    def _():
        h = h_ref[...]
        prev_act = act_ref[...]
        gate = jnp.dot(h, wg_ref[...].astype(BF16), preferred_element_type=F32)
        up = jnp.dot(h, wu_ref[...].astype(BF16), preferred_element_type=F32)
        o_ref[...] += down(prev_act)
        act_ref[...] = (gate * jax.nn.sigmoid(gate) * up).astype(BF16)

    @pl.when(j == last)
    def _():
        f = o_ref[...] + down(act_ref[...])
        o_ref[...] = x_ref[...] + f * _rms_scale(f) * gpost_ref[...]


def _ffn(x, g_pre, w_gate, w_up, w_down, layer, g_post):
    S = x.shape[0]
    tm = min(FFN_TM, S // 2)
    n_chunks = FFN_HIDDEN // FFN_TH
    row_spec = pl.BlockSpec((tm, D_MODEL), lambda i, j: (i, 0))
    up_spec = pl.BlockSpec((None, D_MODEL, FFN_TH), lambda i, j: (layer, 0, jnp.minimum(j, n_chunks - 1)))
    return pl.pallas_call(
        _ffn_kernel,
        grid=(S // tm, n_chunks + 1),
        in_specs=[
            row_spec,
            pl.BlockSpec((1, D_MODEL), lambda i, j: (0, 0)),
            up_spec, up_spec,
            pl.BlockSpec((None, FFN_TH, D_MODEL), lambda i, j: (layer, jnp.maximum(j - 1, 0), 0)),
            pl.BlockSpec((1, D_MODEL), lambda i, j: (0, 0)),
        ],
        out_specs=row_spec,
        out_shape=jax.ShapeDtypeStruct((S, D_MODEL), F32),
        scratch_shapes=[pltpu.VMEM((tm, D_MODEL), BF16), pltpu.VMEM((tm, FFN_TH), BF16)],
        compiler_params=pltpu.CompilerParams(
            dimension_semantics=("arbitrary", "arbitrary"), vmem_limit_bytes=FFN_VMEM_LIMIT),
        name="ffn",
    )(x, g_pre, w_gate, w_up, w_down, g_post)


def _rope_lane_tables(S):
    half = ROT_DIM // 2
    J = S // RUNS
    pos = (np.arange(J)[None, :] * RUNS + np.arange(RUNS)[:, None]).reshape(S).astype(np.float64)
    inv = ROPE_THETA ** (-np.arange(0, ROT_DIM, 2, dtype=np.float64) / ROT_DIM)
    ang = pos[:, None] * inv[None, :]
    cos, sin = np.cos(ang), np.sin(ang)
    ones = np.ones((S, HEAD_DIM - ROT_DIM))
    zeros_h = np.zeros((S, half))
    zeros_r = np.zeros((S, HEAD_DIM - ROT_DIM))
    cos_h = np.concatenate([cos, cos, ones], axis=1)
    sin_lo_h = np.concatenate([-sin, zeros_h, zeros_r], axis=1)
    sin_hi_h = np.concatenate([zeros_h, sin, zeros_r], axis=1)
    pair = lambda t: jnp.asarray(np.concatenate([t, t], axis=1).astype(np.float32))
    return pair(cos_h), pair(sin_lo_h), pair(sin_hi_h)


def kernel(x, w_in, conv_w, conv_b, conv_ln_g, conv_ln_b, pool_w, pool_scale, g_mix, w_out,
           g_pre_mix, g_post_mix, g_pre_ffn, g_post_ffn, w_gate, w_up, w_down):
    B, S, _ = x.shape
    depth = w_in.shape[0]
    assert B == 1, "the sequence-mixing kernels treat all rows as one sequence"
    cos_t, sin_lo_t, sin_hi_t = _rope_lane_tables(S)
    bias = jnp.asarray(_band_bias())
    row = lambda v: v.reshape(1, -1)
    w_in, pool_w = w_in.astype(BF16), pool_w.astype(BF16)

    xb = x.reshape(S, D_MODEL)
    for l in range(depth):
        natural = l == 0
        proj = _in_proj(xb, natural, row(g_pre_mix[l]), w_in, l, cos_t, sin_lo_t, sin_hi_t)
        ya, yc = _local_mix(proj, conv_w[l], row(conv_b[l]), row(conv_ln_g[l]),
                            row(conv_ln_b[l]), pool_w[l], row(pool_scale[l]),
                            row(g_mix[l, :C_A]), row(g_mix[l, C_A + C_B:]))
        yb = _attention(proj, bias)
        xb = _out_proj(ya, yb, yc, row(g_mix[l, C_A:C_A + C_B]), w_out, l, xb, natural, l == depth - 1,
                       row(g_post_mix[l]))
        xb = _ffn(xb, row(g_pre_ffn[l]), w_gate, w_up, w_down, l, row(g_post_ffn[l]))
    return xb.reshape(B, S, D_MODEL)
```

```python
import functools

import numpy as np
import jax
import jax.numpy as jnp
from jax import lax
from jax.experimental import pallas as pl
from jax.experimental.pallas import tpu as pltpu

F32 = jnp.float32
BF16 = jnp.bfloat16

D_MODEL = 2048
C_A = 512
N_HEADS = 16
HEAD_DIM = 64
C_B = N_HEADS * HEAD_DIM
C_C = 512
POOL_WINDOWS = (2, 4, 8, 16)
C_G = C_C // len(POOL_WINDOWS)
MIX_WIDTH = C_A + C_B + C_C
IN_WIDTH = 2 * C_A + 3 * C_B + C_C
CONV_WIDTH = 31
CONV_HALF = CONV_WIDTH // 2
DILATIONS = (1, 4, 16)
HALF_WINDOW = 64
ROT_DIM = HEAD_DIM // 4
ROPE_THETA = 500000.0
FFN_HIDDEN = 5632
EPS = 1e-6
NEG = -1e30

LANES = 128
SUBLANES = 8
N_PAIRS = N_HEADS // 2
RUNS = DILATIONS[-1]
ROW_TILE = 512
VMEM_LIMIT = 52 * 1024 * 1024


def _rms_scale(t):
    return lax.rsqrt(jnp.mean(t * t, axis=-1, keepdims=True) + EPS)


def _row_tile(S):
    tm = min(ROW_TILE, S // RUNS)
    assert (S // RUNS) % tm == 0 and tm % (RUNS * 2 * SUBLANES) == 0
    return tm


def _rows_spec(S, width, mixed, tm=None):
    J, tm = S // RUNS, tm or _row_tile(S)
    if mixed:
        return (RUNS, J, width), pl.BlockSpec((RUNS, tm // RUNS, width), lambda i, *_: (0, i, 0))
    return (S, width), pl.BlockSpec((tm, width), lambda i, *_: (i, 0))


def _natural_spec(S, width, tm=None, **spec_kwargs):
    J, tm = S // RUNS, tm or _row_tile(S)
    return (J, RUNS, width), pl.BlockSpec((tm // RUNS, RUNS, width), lambda i, *_: (i, 0, 0), **spec_kwargs)


def _load_rows(ref):
    v = ref[...]
    return v.reshape(-1, v.shape[-1]) if v.ndim == 3 else v


def _regroup_pitch(rows):
    return rows + SUBLANES if (rows // SUBLANES) % 2 == 0 else rows


def _regroup_by_run(x_ref, by_run_ref):
    for j in range(x_ref.shape[0]):
        by_run_ref[:, j, :] = x_ref[j]


IN_TN = 1536
IN_CHUNK = 256
N_SLABS = IN_WIDTH // LANES
Q_SLAB, K_SLAB, V_SLAB, POOL_SLAB = 8, 16, 24, 32
IN_RUNS_PER_TILE = 2


def _in_proj_kernel(x_ref, g_ref, w_ref, cos_ref, sin_lo_ref, sin_hi_ref, proj_ref, h_ref,
                    *by_run, x_is_natural):
    j = pl.program_id(1)
    tm = h_ref.shape[0]

    @pl.when(j == 0)
    def _():
        if x_is_natural:
            rows = tm // RUNS
            _regroup_by_run(x_ref, by_run[0])
            for r in range(RUNS):
                x = by_run[0][r, 0:rows, :]
                h_ref[r * rows:(r + 1) * rows, :] = (x * _rms_scale(x) * g_ref[...]).astype(BF16)
        else:
            x = x_ref[...]
            h_ref[...] = (x * _rms_scale(x) * g_ref[...]).astype(BF16)

    slabs_per_step = IN_TN // LANES
    for step in range(IN_WIDTH // IN_TN):
        @pl.when(j == step)
        def _(step=step):
            for c in range(IN_TN // IN_CHUNK):
                acc = jnp.dot(h_ref[...], w_ref[:, c * IN_CHUNK:(c + 1) * IN_CHUNK],
                              preferred_element_type=F32)
                for t in range(IN_CHUNK // LANES):
                    local = c * (IN_CHUNK // LANES) + t
                    slab = step * slabs_per_step + local
                    blk = acc[:, t * LANES:(t + 1) * LANES]
                    if Q_SLAB <= slab < V_SLAB:
                        blk = (blk * _load_rows(cos_ref)
                               + pltpu.roll(blk, ROT_DIM // 2, 1) * _load_rows(sin_hi_ref)
                               + pltpu.roll(blk, LANES - ROT_DIM // 2, 1) * _load_rows(sin_lo_ref))
                        if slab < K_SLAB:
                            blk = blk * HEAD_DIM ** -0.5
                    proj_ref[local] = blk.reshape(proj_ref.shape[1:])


def _in_proj(x, x_is_natural, g, w, layer, cos_t, sin_lo_t, sin_hi_t):
    S = x.shape[0]
    J = S // RUNS
    slabs_per_step = IN_TN // LANES
    tm = IN_RUNS_PER_TILE * J
    if x_is_natural:
        view, x_spec = _natural_spec(S, D_MODEL, tm=tm, pipeline_mode=pl.Buffered(1))
        tab_view, tab_spec = _rows_spec(S, LANES, mixed=True, tm=tm)
        out_spec = pl.BlockSpec((slabs_per_step, RUNS, tm // RUNS, LANES), lambda i, j: (j, 0, i, 0))
    else:
        view, x_spec = _rows_spec(S, D_MODEL, mixed=False, tm=tm)
        tab_view, tab_spec = _rows_spec(S, LANES, mixed=False, tm=tm)
        out_spec = pl.BlockSpec((slabs_per_step, IN_RUNS_PER_TILE, J, LANES), lambda i, j: (j, i, 0, 0))
    tables = [t.reshape(tab_view) for t in (cos_t, sin_lo_t, sin_hi_t)]
    return pl.pallas_call(
        functools.partial(_in_proj_kernel, x_is_natural=x_is_natural),
        grid=(S // tm, IN_WIDTH // IN_TN),
        in_specs=[
            x_spec,
            pl.BlockSpec((1, D_MODEL), lambda i, j: (0, 0)),
            pl.BlockSpec((None, D_MODEL, IN_TN), lambda i, j: (layer, 0, j)),
            tab_spec, tab_spec, tab_spec,
        ],
        out_specs=out_spec,
        out_shape=jax.ShapeDtypeStruct((N_SLABS, RUNS, J, LANES), F32),
        scratch_shapes=[pltpu.VMEM((tm, D_MODEL), BF16)] + (
            [pltpu.VMEM((RUNS, _regroup_pitch(tm // RUNS), D_MODEL), F32)] if x_is_natural else []),
        compiler_params=pltpu.CompilerParams(
            dimension_semantics=("arbitrary", "arbitrary"), vmem_limit_bytes=VMEM_LIMIT),
        name="in_proj",
    )(x.reshape(view), g, w, *tables)


MIX_TJ = 32
HALO = SUBLANES


def _tap_source(r, offset):
    q = r + offset
    return q % RUNS, q // RUNS


def _local_mix_kernel(cu_ref, cu_prev_ref, cu_next_ref, pu_ref, pu_prev_ref, pu_next_ref,
                      cw_ref, cb_ref, lng_ref, lnb_ref, pw_ref, ps_ref, ga_ref, gc_ref,
                      ya_ref, yc_ref, hbuf, h_dn, h_up, cbuf, ubuf, u_dn, u_up, pbuf, *, seq_len):
    i = pl.program_id(0)
    n = pl.num_programs(0)
    tj = ya_ref.shape[1]
    not_first = (i > 0).astype(F32)
    not_last = (i < n - 1).astype(F32)
    mid = slice(HALO, HALO + tj)

    def glu(ref, lc, r):
        return ref[lc, r] * jax.nn.sigmoid(ref[C_A // LANES + lc, r])

    n_lc = C_A // LANES
    for r in range(RUNS):
        for lc in range(n_lc):
            cols = slice(lc * LANES, (lc + 1) * LANES)
            hbuf[r, 0:HALO, cols] = glu(cu_prev_ref, lc, r) * not_first
            hbuf[r, mid, cols] = glu(cu_ref, lc, r)
            hbuf[r, HALO + tj:, cols] = glu(cu_next_ref, lc, r) * not_last
        for gi in range(len(POOL_WINDOWS)):
            cols = slice(gi * C_G, (gi + 1) * C_G)
            ubuf[r, 0:HALO, cols] = pu_prev_ref[gi, r] * not_first
            ubuf[r, mid, cols] = pu_ref[gi, r]
            ubuf[r, HALO + tj:, cols] = pu_next_ref[gi, r] * not_last
    for r in range(RUNS):
        h_dn[r] = hbuf[r, HALO - 1:HALO - 1 + tj, :]
        h_up[r] = hbuf[r, HALO + 1:HALO + 1 + tj, :]
        u_dn[r] = ubuf[r, HALO - 1:HALO - 1 + tj, :]
        u_up[r] = ubuf[r, HALO + 1:HALO + 1 + tj, :]

    def shifted(center, dn, up, run, shift, cols):
        if shift == 0:
            return center[run, mid, cols]
        return (dn if shift < 0 else up)[run, :, cols]

    for r in range(RUNS):
        for lc in range(C_A // LANES):
            cols = slice(lc * LANES, (lc + 1) * LANES)
            acc = jnp.zeros((tj, LANES), F32)
            for t in range(CONV_WIDTH):
                run, shift = _tap_source(r, t - CONV_HALF)
                acc = acc + shifted(hbuf, h_dn, h_up, run, shift, cols) * cw_ref[t:t + 1, cols]
            cbuf[r, :, cols] = acc + cb_ref[:, cols]

    for r in range(RUNS):
        c = cbuf[r]
        mu = jnp.mean(c, axis=-1, keepdims=True)
        cc = c - mu
        var = jnp.mean(cc * cc, axis=-1, keepdims=True)
        y = cc * lax.rsqrt(var + EPS) * lng_ref[...] + lnb_ref[...]
        y = y * jax.nn.sigmoid(y)
        ya_ref[r] = (y * _rms_scale(y) * ga_ref[...]).astype(ya_ref.dtype)

    row = lax.broadcasted_iota(jnp.int32, (tj, LANES), 0)
    for r in range(RUNS):
        pos = (i * tj + row) * RUNS + r
        for gi, win in enumerate(POOL_WINDOWS):
            cols = slice(gi * C_G, (gi + 1) * C_G)
            half = win // 2
            tot = jnp.zeros((tj, LANES), F32)
            for o in range(-half, win - half):
                run, shift = _tap_source(r, o)
                tot = tot + shifted(ubuf, u_dn, u_up, run, shift, cols)
            lo = jnp.maximum(pos - half, 0)
            hi = jnp.minimum(pos + (win - half), seq_len)
            pbuf[r * tj:(r + 1) * tj, cols] = tot / (hi - lo).astype(F32) - ubuf[r, mid, cols]

    mixed = []
    for gi in range(len(POOL_WINDOWS)):
        cols = slice(gi * C_G, (gi + 1) * C_G)
        m = jnp.dot(pbuf[:, cols].astype(BF16), pw_ref[gi], preferred_element_type=F32)
        mixed.append(m * ps_ref[:, cols])
    ssq = sum(jnp.sum(m * m, axis=-1, keepdims=True) for m in mixed)
    scale = lax.rsqrt(ssq * (1.0 / C_C) + EPS)
    for gi, m in enumerate(mixed):
        cols = slice(gi * C_G, (gi + 1) * C_G)
        y = (m * scale * gc_ref[:, cols]).astype(yc_ref.dtype)
        for r in range(RUNS):
            yc_ref[r, :, cols] = y[r * tj:(r + 1) * tj]


def _local_mix(proj, conv_w, conv_b, ln_g, ln_b, pool_w, pool_scale, g_a, g_c):
    J = proj.shape[2]
    S = RUNS * J
    tj = min(MIX_TJ, J)
    hb = tj // HALO
    last_hb = J // HALO - 1
    conv_slabs, pool_slabs = 2 * C_A // LANES, C_C // LANES
    pool_blk = POOL_SLAB // pool_slabs
    cur_map = lambda i: (0, i, 0)
    const2 = lambda i: (0, 0)
    slab_specs = lambda n, first: [
        pl.BlockSpec((n, RUNS, tj, LANES), lambda i: (first, 0, i, 0)),
        pl.BlockSpec((n, RUNS, HALO, LANES), lambda i: (first, 0, jnp.maximum(i * hb - 1, 0), 0)),
        pl.BlockSpec((n, RUNS, HALO, LANES), lambda i: (first, 0, jnp.minimum((i + 1) * hb, last_hb), 0)),
    ]
    ya, yc = pl.pallas_call(
        functools.partial(_local_mix_kernel, seq_len=S),
        grid=(J // tj,),
        in_specs=slab_specs(conv_slabs, 0) + slab_specs(pool_slabs, pool_blk) + [
            pl.BlockSpec((CONV_WIDTH, C_A), const2),
            pl.BlockSpec((1, C_A), const2),
            pl.BlockSpec((1, C_A), const2),
            pl.BlockSpec((1, C_A), const2),
            pl.BlockSpec((len(POOL_WINDOWS), C_G, C_G), lambda i: (0, 0, 0)),
            pl.BlockSpec((1, C_C), const2),
            pl.BlockSpec((1, C_A), const2),
            pl.BlockSpec((1, C_C), const2),
        ],
        out_specs=[pl.BlockSpec((RUNS, tj, C_A), cur_map), pl.BlockSpec((RUNS, tj, C_C), cur_map)],
        out_shape=[jax.ShapeDtypeStruct((RUNS, J, C_A), BF16),
                   jax.ShapeDtypeStruct((RUNS, J, C_C), BF16)],
        scratch_shapes=[
            pltpu.VMEM((RUNS, tj + 2 * HALO, C_A), F32),
            pltpu.VMEM((RUNS, tj, C_A), F32),
            pltpu.VMEM((RUNS, tj, C_A), F32),
            pltpu.VMEM((RUNS, tj, C_A), F32),
            pltpu.VMEM((RUNS, tj + 2 * HALO, C_C), F32),
            pltpu.VMEM((RUNS, tj, C_C), F32),
            pltpu.VMEM((RUNS, tj, C_C), F32),
            pltpu.VMEM((RUNS * tj, C_C), F32),
        ],
        compiler_params=pltpu.CompilerParams(
            dimension_semantics=("arbitrary",), vmem_limit_bytes=VMEM_LIMIT),
        name="local_mix",
    )(proj, proj, proj, proj, proj, proj,
      conv_w, conv_b, ln_g, ln_b, pool_w, pool_scale, g_a, g_c)
    return ya.reshape(S, C_A), yc.reshape(S, C_C)


ATT_TQ = 128
ATT_W = ATT_TQ + 2 * HALF_WINDOW
ATT_UNROLL = 16
INTERIOR, FIRST, LAST = 0, 1, 2


def _block_layout(d, kind):
    pieces = RUNS // d
    q_rows, k_rows = ATT_TQ // pieces, ATT_W // pieces
    first_group = {INTERIOR: -HALF_WINDOW // pieces, FIRST: 0, LAST: (ATT_TQ - ATT_W) // pieces}[kind]
    a_q, u_q = np.divmod(np.arange(ATT_TQ), q_rows)
    a_k, u_k = np.divmod(np.arange(ATT_W), k_rows)
    return pieces * u_q + a_q, pieces * (u_k + first_group) + a_k, first_group


def _band_bias():
    tables = []
    for d in DILATIONS:
        for kind in (INTERIOR, FIRST, LAST):
            q_step, k_step, _ = _block_layout(d, kind)
            ok = np.abs(k_step[None, :] - q_step[:, None]) <= HALF_WINDOW
            tables.append(np.where(ok, 0.0, NEG))
    return np.stack(tables).astype(np.float32)


def _attention_kernel(q_ref, k_ref, v_ref, bias_ref, o_ref, m_acc, s_acc):
    S = o_ref.shape[0]
    J = S // RUNS
    lane = lax.broadcasted_iota(jnp.int32, (ATT_TQ, LANES), 1)
    head_a = lane < HEAD_DIM
    nt_dims = (((1,), (1,)), ((), ()))

    def gather(ref, starts, rows):
        return jnp.concatenate([ref[pl.ds(s, rows), :] for s in starts], axis=0)

    def attend(q, kw, vw, bias):
        q2, kw, vw = q.astype(BF16), kw.astype(BF16), vw.astype(BF16)
        zero = jnp.zeros_like(q2)

        def one_head(qh):
            s = lax.dot_general(qh, kw, nt_dims, preferred_element_type=F32) + bias
            m = jnp.max(s, axis=-1, keepdims=True)
            p = jnp.exp(s - m)
            l = jnp.sum(p, axis=-1, keepdims=True)
            o = jnp.dot(p.astype(BF16), vw, preferred_element_type=F32)
            return m, l, o

        ma, la, oa = one_head(jnp.where(head_a, q2, zero))
        mb, lb, ob = one_head(jnp.where(head_a, zero, q2))
        return jnp.where(head_a, ma, mb), jnp.where(head_a, la, lb), jnp.where(head_a, oa, ob)

    def block(pattern, kind, q_starts, k_starts, bias):
        pieces = len(q_starts)
        q_rows, k_rows = ATT_TQ // pieces, ATT_W // pieces
        q = gather(q_ref.at[0], q_starts, q_rows)
        kw = gather(k_ref.at[0], k_starts, k_rows)
        vw = gather(v_ref.at[0], k_starts, k_rows)
        m_blk, l_blk, o_blk = attend(q, kw, vw, bias)
        if pattern > 0:
            m_old = gather(m_acc, q_starts, q_rows)
            m_new = jnp.maximum(m_old, m_blk)
            w_old = jnp.exp(m_old - m_new)
            w_blk = jnp.exp(m_blk - m_new)
            l_blk = w_old * gather(s_acc, q_starts, q_rows) + w_blk * l_blk
            o_blk = w_old * gather(o_ref, q_starts, q_rows) + w_blk * o_blk
            m_blk = m_new
        for a, s in enumerate(q_starts):
            rows = slice(a * q_rows, (a + 1) * q_rows)
            m_acc[pl.ds(s, q_rows), :] = m_blk[rows]
            s_acc[pl.ds(s, q_rows), :] = l_blk[rows]
            o_ref[pl.ds(s, q_rows), :] = o_blk[rows]

    def aligned(x):
        if isinstance(x, int):
            assert x % SUBLANES == 0
            return x
        return pl.multiple_of(x, SUBLANES)

    for pattern, d in enumerate(DILATIONS):
        pieces = RUNS // d
        q_rows, k_rows = ATT_TQ // pieces, ATT_W // pieces
        n_blk = (S // d) // ATT_TQ
        assert n_blk >= 2, "first and last key blocks must be distinct"
        first_groups = [_block_layout(d, kind)[2] for kind in (INTERIOR, FIRST, LAST)]

        def run_block(sub, jb, kind, pattern=pattern, d=d, pieces=pieces, q_rows=q_rows,
                      first_groups=first_groups):
            g0 = jb * q_rows
            q_starts = [aligned((sub + d * a) * J + g0) for a in range(pieces)]
            if kind == INTERIOR and first_groups[kind] % SUBLANES:
                lead = SUBLANES - (-first_groups[kind]) % SUBLANES
                k_starts = [aligned((sub + d * a) * J + g0 - lead + first_groups[kind]) + lead
                            for a in range(pieces)]
            else:
                k_starts = [aligned((sub + d * a) * J + g0 + first_groups[kind]) for a in range(pieces)]
            block(pattern, kind, q_starts, k_starts, bias_ref[3 * pattern + kind])

        def kind_of(jb, n_blk=n_blk):
            return FIRST if jb == 0 else LAST if jb == n_blk - 1 else INTERIOR

        group = ATT_UNROLL
        if n_blk <= group:
            subs_per_trip = group // n_blk

            def short_subs(t, carry, run_block=run_block, n_blk=n_blk, subs_per_trip=subs_per_trip,
                           kind_of=kind_of):
                for k in range(subs_per_trip):
                    for jb in range(n_blk):
                        run_block(t * subs_per_trip + k, jb, kind_of(jb))
                return carry

            assert d % subs_per_trip == 0
            lax.fori_loop(0, d // subs_per_trip, short_subs, 0)
        else:
            n_mid = n_blk - 2 * group
            assert n_mid >= 0 and n_mid % group == 0

            def long_sub(sub, carry, run_block=run_block, n_blk=n_blk, n_mid=n_mid, kind_of=kind_of,
                         group=group):
                for jb in range(group):
                    run_block(sub, jb, kind_of(jb))

                def middle(t, c):
                    for k in range(group):
                        run_block(sub, group + t * group + k, INTERIOR)
                    return c

                if n_mid:
                    lax.fori_loop(0, n_mid // group, middle, 0)
                for jb in range(n_blk - group, n_blk):
                    run_block(sub, jb, kind_of(jb))
                return carry

            if d == 1:
                long_sub(0, 0)
            else:
                lax.fori_loop(0, d, long_sub, 0)

    def normalise(c, carry):
        rows = pl.ds(pl.multiple_of(c * ATT_W, ATT_W), ATT_W)
        o_ref[rows, :] = o_ref[rows, :] / s_acc[rows, :]
        return carry

    lax.fori_loop(0, S // ATT_W, normalise, 0)


def _attention(proj, bias):
    S = proj.shape[1] * proj.shape[2]
    qkv = proj.reshape(N_SLABS, S, LANES)
    qkv_spec = lambda off: pl.BlockSpec((1, S, LANES), lambda p: (p + off, 0, 0))
    return pl.pallas_call(
        _attention_kernel,
        grid=(N_PAIRS,),
        in_specs=[
            qkv_spec(Q_SLAB), qkv_spec(K_SLAB), qkv_spec(V_SLAB),
            pl.BlockSpec(bias.shape, lambda p: (0, 0, 0)),
        ],
        out_specs=pl.BlockSpec((S, LANES), lambda p: (0, p)),
        out_shape=jax.ShapeDtypeStruct((S, C_B), F32),
        scratch_shapes=[pltpu.VMEM((S, LANES), F32), pltpu.VMEM((S, LANES), F32)],
        compiler_params=pltpu.CompilerParams(
            dimension_semantics=("arbitrary",), vmem_limit_bytes=VMEM_LIMIT),
        name="attention",
    )(qkv, qkv, qkv, bias)


OUT_CAST_ROWS = 256
OUT_VMEM_LIMIT = 58 * 1024 * 1024


def _out_proj_kernel(ya_ref, yb_ref, yc_ref, gb_ref, w_ref, x_ref, gp_ref, o_ref, wb_ref, *by_run,
                     x_is_natural, out_is_natural):
    @pl.when(pl.program_id(0) == 0)
    def _():
        for c in range(MIX_WIDTH // OUT_CAST_ROWS):
            rows = slice(c * OUT_CAST_ROWS, (c + 1) * OUT_CAST_ROWS)
            wb_ref[rows, :] = w_ref[rows, :].astype(BF16)

    yb = _load_rows(yb_ref)
    ybn = (yb * _rms_scale(yb) * gb_ref[...]).astype(BF16)
    acc = jnp.dot(_load_rows(ya_ref), wb_ref[0:C_A, :], preferred_element_type=F32)
    acc = acc + jnp.dot(ybn, wb_ref[C_A:C_A + C_B, :], preferred_element_type=F32)
    acc = acc + jnp.dot(_load_rows(yc_ref), wb_ref[C_A + C_B:, :], preferred_element_type=F32)
    update = acc * _rms_scale(acc) * gp_ref[...]
    if x_is_natural:
        rows = update.shape[0] // RUNS
        _regroup_by_run(x_ref, by_run[0])
        for r in range(RUNS):
            o_ref[r] = by_run[0][r, 0:rows, :] + update[r * rows:(r + 1) * rows]
    elif out_is_natural:
        y = _load_rows(x_ref) + update
        rows = y.shape[0] // RUNS
        for r in range(RUNS):
            o_ref[:, r, :] = y[r * rows:(r + 1) * rows]
    else:
        o_ref[...] = x_ref[...] + update


def _out_proj(ya, yb, yc, g_b, w, layer, x, x_is_natural, out_is_natural, g_post):
    S = ya.shape[0]
    tm = _row_tile(S)
    const2 = lambda i: (0, 0)
    assert not (x_is_natural and out_is_natural)
    mixed = x_is_natural or out_is_natural
    ya_view, ya_spec = _rows_spec(S, C_A, mixed=mixed)
    yb_view, yb_spec = _rows_spec(S, C_B, mixed=mixed)
    yc_view, yc_spec = _rows_spec(S, C_C, mixed=mixed)
    res_view, res_spec = _rows_spec(S, D_MODEL, mixed=mixed)
    x_view, x_spec = _natural_spec(S, D_MODEL) if x_is_natural else (res_view, res_spec)
    out_view, out_spec = _natural_spec(S, D_MODEL) if out_is_natural else (res_view, res_spec)
    out = pl.pallas_call(
        functools.partial(_out_proj_kernel, x_is_natural=x_is_natural, out_is_natural=out_is_natural),
        grid=(S // tm,),
        in_specs=[
            ya_spec, yb_spec, yc_spec,
            pl.BlockSpec((1, C_B), const2),
            pl.BlockSpec((None, MIX_WIDTH, D_MODEL), lambda i: (layer, 0, 0), pipeline_mode=pl.Buffered(1)),
            x_spec,
            pl.BlockSpec((1, D_MODEL), const2),
        ],
        out_specs=out_spec,
        out_shape=jax.ShapeDtypeStruct(out_view, F32),
        scratch_shapes=[pltpu.VMEM((MIX_WIDTH, D_MODEL), BF16)] + (
            [pltpu.VMEM((RUNS, _regroup_pitch(tm // RUNS), D_MODEL), F32)] if x_is_natural else []),
        compiler_params=pltpu.CompilerParams(
            dimension_semantics=("arbitrary",), vmem_limit_bytes=OUT_VMEM_LIMIT),
        name="out_proj",
    )(ya.reshape(ya_view), yb.reshape(yb_view), yc.reshape(yc_view), g_b, w, x.reshape(x_view), g_post)
    return out.reshape(S, D_MODEL)


FFN_TM = 1024
FFN_TH = 256
FFN_VMEM_LIMIT = 58 * 1024 * 1024


def _ffn_kernel(x_ref, gpre_ref, wg_ref, wu_ref, wd_ref, gpost_ref, o_ref, h_ref):
    j = pl.program_id(1)

    @pl.when(j == 0)
    def _():
        x = x_ref[...]
        h_ref[...] = (x * _rms_scale(x) * gpre_ref[...]).astype(BF16)
        o_ref[...] = jnp.zeros_like(o_ref)

    h = h_ref[...]
    gate = jnp.dot(h, wg_ref[...].astype(BF16), preferred_element_type=F32)
    up = jnp.dot(h, wu_ref[...].astype(BF16), preferred_element_type=F32)
    act = (gate * jax.nn.sigmoid(gate) * up).astype(BF16)
    o_ref[...] += jnp.dot(act, wd_ref[...].astype(BF16), preferred_element_type=F32)

    @pl.when(j == pl.num_programs(1) - 1)
    def _():
        f = o_ref[...]
        o_ref[...] = x_ref[...] + f * _rms_scale(f) * gpost_ref[...]


def _ffn(x, g_pre, w_gate, w_up, w_down, layer, g_post):
    S = x.shape[0]
    tm = min(FFN_TM, S // 2)
    row_spec = pl.BlockSpec((tm, D_MODEL), lambda i, j: (i, 0))
    return pl.pallas_call(
        _ffn_kernel,
        grid=(S // tm, FFN_HIDDEN // FFN_TH),
        in_specs=[
            row_spec,
            pl.BlockSpec((1, D_MODEL), lambda i, j: (0, 0)),
            pl.BlockSpec((None, D_MODEL, FFN_TH), lambda i, j: (layer, 0, j)),
            pl.BlockSpec((None, D_MODEL, FFN_TH), lambda i, j: (layer, 0, j)),
            pl.BlockSpec((None, FFN_TH, D_MODEL), lambda i, j: (layer, j, 0)),
            pl.BlockSpec((1, D_MODEL), lambda i, j: (0, 0)),
        ],
        out_specs=row_spec,
        out_shape=jax.ShapeDtypeStruct((S, D_MODEL), F32),
        scratch_shapes=[pltpu.VMEM((tm, D_MODEL), BF16)],
        compiler_params=pltpu.CompilerParams(
            dimension_semantics=("arbitrary", "arbitrary"), vmem_limit_bytes=FFN_VMEM_LIMIT),
        name="ffn",
    )(x, g_pre, w_gate, w_up, w_down, g_post)


def _rope_lane_tables(S):
    half = ROT_DIM // 2
    J = S // RUNS
    pos = (np.arange(J)[None, :] * RUNS + np.arange(RUNS)[:, None]).reshape(S).astype(np.float64)
    inv = ROPE_THETA ** (-np.arange(0, ROT_DIM, 2, dtype=np.float64) / ROT_DIM)
    ang = pos[:, None] * inv[None, :]
    cos, sin = np.cos(ang), np.sin(ang)
    ones = np.ones((S, HEAD_DIM - ROT_DIM))
    zeros_h = np.zeros((S, half))
    zeros_r = np.zeros((S, HEAD_DIM - ROT_DIM))
    cos_h = np.concatenate([cos, cos, ones], axis=1)
    sin_lo_h = np.concatenate([-sin, zeros_h, zeros_r], axis=1)
    sin_hi_h = np.concatenate([zeros_h, sin, zeros_r], axis=1)
    pair = lambda t: jnp.asarray(np.concatenate([t, t], axis=1).astype(np.float32))
    return pair(cos_h), pair(sin_lo_h), pair(sin_hi_h)


def kernel(x, w_in, conv_w, conv_b, conv_ln_g, conv_ln_b, pool_w, pool_scale, g_mix, w_out,
           g_pre_mix, g_post_mix, g_pre_ffn, g_post_ffn, w_gate, w_up, w_down):
    B, S, _ = x.shape
    depth = w_in.shape[0]
    assert B == 1, "the sequence-mixing kernels treat all rows as one sequence"
    cos_t, sin_lo_t, sin_hi_t = _rope_lane_tables(S)
    bias = jnp.asarray(_band_bias())
    row = lambda v: v.reshape(1, -1)
    w_in, pool_w = w_in.astype(BF16), pool_w.astype(BF16)

    xb = x.reshape(S, D_MODEL)
    for l in range(depth):
        natural = l == 0
        proj = _in_proj(xb, natural, row(g_pre_mix[l]), w_in, l, cos_t, sin_lo_t, sin_hi_t)
        ya, yc = _local_mix(proj, conv_w[l], row(conv_b[l]), row(conv_ln_g[l]),
                            row(conv_ln_b[l]), pool_w[l], row(pool_scale[l]),
                            row(g_mix[l, :C_A]), row(g_mix[l, C_A + C_B:]))
        yb = _attention(proj, bias)
        xb = _out_proj(ya, yb, yc, row(g_mix[l, C_A:C_A + C_B]), w_out, l, xb, natural, l == depth - 1,
                       row(g_post_mix[l]))
        xb = _ffn(xb, row(g_pre_ffn[l]), w_gate, w_up, w_down, l, row(g_post_ffn[l]))
    return xb.reshape(B, S, D_MODEL)
```

```python
import functools

import numpy as np
import jax
import jax.numpy as jnp
from jax import lax
from jax.experimental import pallas as pl
from jax.experimental.pallas import tpu as pltpu

F32 = jnp.float32
BF16 = jnp.bfloat16

D_MODEL = 2048
C_A = 512
N_HEADS = 16
HEAD_DIM = 64
C_B = N_HEADS * HEAD_DIM
C_C = 512
POOL_WINDOWS = (2, 4, 8, 16)
C_G = C_C // len(POOL_WINDOWS)
MIX_WIDTH = C_A + C_B + C_C
IN_WIDTH = 2 * C_A + 3 * C_B + C_C
CONV_WIDTH = 31
CONV_HALF = CONV_WIDTH // 2
DILATIONS = (1, 4, 16)
HALF_WINDOW = 64
ROT_DIM = HEAD_DIM // 4
ROPE_THETA = 500000.0
FFN_HIDDEN = 5632
EPS = 1e-6
NEG = -1e30

LANES = 128
SUBLANES = 8
N_PAIRS = N_HEADS // 2
RUNS = DILATIONS[-1]
ROW_TILE = 512
VMEM_LIMIT = 52 * 1024 * 1024


def _rms_scale(t):
    return lax.rsqrt(jnp.mean(t * t, axis=-1, keepdims=True) + EPS)


def _row_tile(S):
    tm = min(ROW_TILE, S // RUNS)
    assert (S // RUNS) % tm == 0 and tm % (RUNS * 2 * SUBLANES) == 0
    return tm


def _rows_spec(S, width, mixed, tm=None):
    J, tm = S // RUNS, tm or _row_tile(S)
    if mixed:
        return (RUNS, J, width), pl.BlockSpec((RUNS, tm // RUNS, width), lambda i, *_: (0, i, 0))
    return (S, width), pl.BlockSpec((tm, width), lambda i, *_: (i, 0))


def _natural_spec(S, width, tm=None, **spec_kwargs):
    J, tm = S // RUNS, tm or _row_tile(S)
    return (J, RUNS, width), pl.BlockSpec((tm // RUNS, RUNS, width), lambda i, *_: (i, 0, 0), **spec_kwargs)


def _load_rows(ref):
    v = ref[...]
    return v.reshape(-1, v.shape[-1]) if v.ndim == 3 else v


def _regroup_pitch(rows):
    return rows + SUBLANES if (rows // SUBLANES) % 2 == 0 else rows


def _regroup_by_run(x_ref, by_run_ref):
    for j in range(x_ref.shape[0]):
        by_run_ref[:, j, :] = x_ref[j]


IN_TN = 1536
IN_CHUNK = 256
N_SLABS = IN_WIDTH // LANES
Q_SLAB, K_SLAB, V_SLAB, POOL_SLAB = 8, 16, 24, 32
IN_RUNS_PER_TILE = 2


def _in_proj_kernel(x_ref, g_ref, w_ref, cos_ref, sin_lo_ref, sin_hi_ref, proj_ref, h_ref,
                    *by_run, x_is_natural):
    j = pl.program_id(1)
    tm = h_ref.shape[0]

    @pl.when(j == 0)
    def _():
        if x_is_natural:
            rows = tm // RUNS
            _regroup_by_run(x_ref, by_run[0])
            for r in range(RUNS):
                x = by_run[0][r, 0:rows, :]
                h_ref[r * rows:(r + 1) * rows, :] = (x * _rms_scale(x) * g_ref[...]).astype(BF16)
        else:
            x = x_ref[...]
            h_ref[...] = (x * _rms_scale(x) * g_ref[...]).astype(BF16)

    slabs_per_step = IN_TN // LANES
    for step in range(IN_WIDTH // IN_TN):
        @pl.when(j == step)
        def _(step=step):
            for c in range(IN_TN // IN_CHUNK):
                acc = jnp.dot(h_ref[...], w_ref[:, c * IN_CHUNK:(c + 1) * IN_CHUNK],
                              preferred_element_type=F32)
                for t in range(IN_CHUNK // LANES):
                    local = c * (IN_CHUNK // LANES) + t
                    slab = step * slabs_per_step + local
                    blk = acc[:, t * LANES:(t + 1) * LANES]
                    if Q_SLAB <= slab < V_SLAB:
                        blk = (blk * _load_rows(cos_ref)
                               + pltpu.roll(blk, ROT_DIM // 2, 1) * _load_rows(sin_hi_ref)
                               + pltpu.roll(blk, LANES - ROT_DIM // 2, 1) * _load_rows(sin_lo_ref))
                        if slab < K_SLAB:
                            blk = blk * HEAD_DIM ** -0.5
                    proj_ref[local] = blk.reshape(proj_ref.shape[1:])


def _in_proj(x, x_is_natural, g, w, layer, cos_t, sin_lo_t, sin_hi_t):
    S = x.shape[0]
    J = S // RUNS
    slabs_per_step = IN_TN // LANES
    tm = IN_RUNS_PER_TILE * J
    if x_is_natural:
        view, x_spec = _natural_spec(S, D_MODEL, tm=tm, pipeline_mode=pl.Buffered(1))
        tab_view, tab_spec = _rows_spec(S, LANES, mixed=True, tm=tm)
        out_spec = pl.BlockSpec((slabs_per_step, RUNS, tm // RUNS, LANES), lambda i, j: (j, 0, i, 0))
    else:
        view, x_spec = _rows_spec(S, D_MODEL, mixed=False, tm=tm)
        tab_view, tab_spec = _rows_spec(S, LANES, mixed=False, tm=tm)
        out_spec = pl.BlockSpec((slabs_per_step, IN_RUNS_PER_TILE, J, LANES), lambda i, j: (j, i, 0, 0))
    tables = [t.reshape(tab_view) for t in (cos_t, sin_lo_t, sin_hi_t)]
    return pl.pallas_call(
        functools.partial(_in_proj_kernel, x_is_natural=x_is_natural),
        grid=(S // tm, IN_WIDTH // IN_TN),
        in_specs=[
            x_spec,
            pl.BlockSpec((1, D_MODEL), lambda i, j: (0, 0)),
            pl.BlockSpec((None, D_MODEL, IN_TN), lambda i, j: (layer, 0, j)),
            tab_spec, tab_spec, tab_spec,
        ],
        out_specs=out_spec,
        out_shape=jax.ShapeDtypeStruct((N_SLABS, RUNS, J, LANES), F32),
        scratch_shapes=[pltpu.VMEM((tm, D_MODEL), BF16)] + (
            [pltpu.VMEM((RUNS, _regroup_pitch(tm // RUNS), D_MODEL), F32)] if x_is_natural else []),
        compiler_params=pltpu.CompilerParams(
            dimension_semantics=("arbitrary", "arbitrary"), vmem_limit_bytes=VMEM_LIMIT),
        name="in_proj",
    )(x.reshape(view), g, w, *tables)


MIX_TJ = 32
HALO = SUBLANES


def _tap_source(r, offset):
    q = r + offset
    return q % RUNS, q // RUNS


def _local_mix_kernel(cu_ref, cu_prev_ref, cu_next_ref, pu_ref, pu_prev_ref, pu_next_ref,
                      cw_ref, cb_ref, lng_ref, lnb_ref, pw_ref, ps_ref, ga_ref, gc_ref,
                      ya_ref, yc_ref, hbuf, h_dn, h_up, cbuf, ubuf, u_dn, u_up, pbuf, *, seq_len):
    i = pl.program_id(0)
    n = pl.num_programs(0)
    tj = ya_ref.shape[1]
    not_first = (i > 0).astype(F32)
    not_last = (i < n - 1).astype(F32)
    mid = slice(HALO, HALO + tj)

    def glu(ref, lc, r):
        return ref[lc, r] * jax.nn.sigmoid(ref[C_A // LANES + lc, r])

    n_lc = C_A // LANES
    for r in range(RUNS):
        for lc in range(n_lc):
            cols = slice(lc * LANES, (lc + 1) * LANES)
            hbuf[r, 0:HALO, cols] = glu(cu_prev_ref, lc, r) * not_first
            hbuf[r, mid, cols] = glu(cu_ref, lc, r)
            hbuf[r, HALO + tj:, cols] = glu(cu_next_ref, lc, r) * not_last
        for gi in range(len(POOL_WINDOWS)):
            cols = slice(gi * C_G, (gi + 1) * C_G)
            ubuf[r, 0:HALO, cols] = pu_prev_ref[gi, r] * not_first
            ubuf[r, mid, cols] = pu_ref[gi, r]
            ubuf[r, HALO + tj:, cols] = pu_next_ref[gi, r] * not_last
    for r in range(RUNS):
        h_dn[r] = hbuf[r, HALO - 1:HALO - 1 + tj, :]
        h_up[r] = hbuf[r, HALO + 1:HALO + 1 + tj, :]
        u_dn[r] = ubuf[r, HALO - 1:HALO - 1 + tj, :]
        u_up[r] = ubuf[r, HALO + 1:HALO + 1 + tj, :]

    def shifted(center, dn, up, run, shift, cols):
        if shift == 0:
            return center[run, mid, cols]
        return (dn if shift < 0 else up)[run, :, cols]

    for r in range(RUNS):
        for lc in range(C_A // LANES):
            cols = slice(lc * LANES, (lc + 1) * LANES)
            acc = jnp.zeros((tj, LANES), F32)
            for t in range(CONV_WIDTH):
                run, shift = _tap_source(r, t - CONV_HALF)
                acc = acc + shifted(hbuf, h_dn, h_up, run, shift, cols) * cw_ref[t:t + 1, cols]
            cbuf[r, :, cols] = acc + cb_ref[:, cols]

    for r in range(RUNS):
        c = cbuf[r]
        mu = jnp.mean(c, axis=-1, keepdims=True)
        cc = c - mu
        var = jnp.mean(cc * cc, axis=-1, keepdims=True)
        y = cc * lax.rsqrt(var + EPS) * lng_ref[...] + lnb_ref[...]
        y = y * jax.nn.sigmoid(y)
        ya_ref[r] = (y * _rms_scale(y) * ga_ref[...]).astype(ya_ref.dtype)

    row = lax.broadcasted_iota(jnp.int32, (tj, LANES), 0)
    for r in range(RUNS):
        pos = (i * tj + row) * RUNS + r
        for gi, win in enumerate(POOL_WINDOWS):
            cols = slice(gi * C_G, (gi + 1) * C_G)
            half = win // 2
            tot = jnp.zeros((tj, LANES), F32)
            for o in range(-half, win - half):
                run, shift = _tap_source(r, o)
                tot = tot + shifted(ubuf, u_dn, u_up, run, shift, cols)
            lo = jnp.maximum(pos - half, 0)
            hi = jnp.minimum(pos + (win - half), seq_len)
            pbuf[r * tj:(r + 1) * tj, cols] = tot / (hi - lo).astype(F32) - ubuf[r, mid, cols]

    mixed = []
    for gi in range(len(POOL_WINDOWS)):
        cols = slice(gi * C_G, (gi + 1) * C_G)
        m = jnp.dot(pbuf[:, cols].astype(BF16), pw_ref[gi], preferred_element_type=F32)
        mixed.append(m * ps_ref[:, cols])
    ssq = sum(jnp.sum(m * m, axis=-1, keepdims=True) for m in mixed)
    scale = lax.rsqrt(ssq * (1.0 / C_C) + EPS)
    for gi, m in enumerate(mixed):
        cols = slice(gi * C_G, (gi + 1) * C_G)
        y = (m * scale * gc_ref[:, cols]).astype(yc_ref.dtype)
        for r in range(RUNS):
            yc_ref[r, :, cols] = y[r * tj:(r + 1) * tj]


def _local_mix(proj, conv_w, conv_b, ln_g, ln_b, pool_w, pool_scale, g_a, g_c):
    J = proj.shape[2]
    S = RUNS * J
    tj = min(MIX_TJ, J)
    hb = tj // HALO
    last_hb = J // HALO - 1
    conv_slabs, pool_slabs = 2 * C_A // LANES, C_C // LANES
    pool_blk = POOL_SLAB // pool_slabs
    cur_map = lambda i: (0, i, 0)
    const2 = lambda i: (0, 0)
    slab_specs = lambda n, first: [
        pl.BlockSpec((n, RUNS, tj, LANES), lambda i: (first, 0, i, 0)),
        pl.BlockSpec((n, RUNS, HALO, LANES), lambda i: (first, 0, jnp.maximum(i * hb - 1, 0), 0)),
        pl.BlockSpec((n, RUNS, HALO, LANES), lambda i: (first, 0, jnp.minimum((i + 1) * hb, last_hb), 0)),
    ]
    ya, yc = pl.pallas_call(
        functools.partial(_local_mix_kernel, seq_len=S),
        grid=(J // tj,),
        in_specs=slab_specs(conv_slabs, 0) + slab_specs(pool_slabs, pool_blk) + [
            pl.BlockSpec((CONV_WIDTH, C_A), const2),
            pl.BlockSpec((1, C_A), const2),
            pl.BlockSpec((1, C_A), const2),
            pl.BlockSpec((1, C_A), const2),
            pl.BlockSpec((len(POOL_WINDOWS), C_G, C_G), lambda i: (0, 0, 0)),
            pl.BlockSpec((1, C_C), const2),
            pl.BlockSpec((1, C_A), const2),
            pl.BlockSpec((1, C_C), const2),
        ],
        out_specs=[pl.BlockSpec((RUNS, tj, C_A), cur_map), pl.BlockSpec((RUNS, tj, C_C), cur_map)],
        out_shape=[jax.ShapeDtypeStruct((RUNS, J, C_A), BF16),
                   jax.ShapeDtypeStruct((RUNS, J, C_C), BF16)],
        scratch_shapes=[
            pltpu.VMEM((RUNS, tj + 2 * HALO, C_A), F32),
            pltpu.VMEM((RUNS, tj, C_A), F32),
            pltpu.VMEM((RUNS, tj, C_A), F32),
            pltpu.VMEM((RUNS, tj, C_A), F32),
            pltpu.VMEM((RUNS, tj + 2 * HALO, C_C), F32),
            pltpu.VMEM((RUNS, tj, C_C), F32),
            pltpu.VMEM((RUNS, tj, C_C), F32),
            pltpu.VMEM((RUNS * tj, C_C), F32),
        ],
        compiler_params=pltpu.CompilerParams(
            dimension_semantics=("arbitrary",), vmem_limit_bytes=VMEM_LIMIT),
        name="local_mix",
    )(proj, proj, proj, proj, proj, proj,
      conv_w, conv_b, ln_g, ln_b, pool_w, pool_scale, g_a, g_c)
    return ya.reshape(S, C_A), yc.reshape(S, C_C)


ATT_TQ = 128
ATT_W = ATT_TQ + 2 * HALF_WINDOW
ATT_UNROLL = 16
INTERIOR, FIRST, LAST = 0, 1, 2


def _block_layout(d, kind):
    pieces = RUNS // d
    q_rows, k_rows = ATT_TQ // pieces, ATT_W // pieces
    first_group = {INTERIOR: -HALF_WINDOW // pieces, FIRST: 0, LAST: (ATT_TQ - ATT_W) // pieces}[kind]
    a_q, u_q = np.divmod(np.arange(ATT_TQ), q_rows)
    a_k, u_k = np.divmod(np.arange(ATT_W), k_rows)
    return pieces * u_q + a_q, pieces * (u_k + first_group) + a_k, first_group


def _band_bias():
    tables = []
    for d in DILATIONS:
        for kind in (INTERIOR, FIRST, LAST):
            q_step, k_step, _ = _block_layout(d, kind)
            ok = np.abs(k_step[None, :] - q_step[:, None]) <= HALF_WINDOW
            tables.append(np.where(ok, 0.0, NEG))
    return np.stack(tables).astype(np.float32)


def _attention_kernel(q_ref, k_ref, v_ref, bias_ref, o_ref, m_acc, s_acc):
    S = o_ref.shape[0]
    J = S // RUNS
    lane = lax.broadcasted_iota(jnp.int32, (ATT_TQ, LANES), 1)
    head_a = lane < HEAD_DIM
    nt_dims = (((1,), (1,)), ((), ()))

    def gather(ref, starts, rows):
        return jnp.concatenate([ref[pl.ds(s, rows), :] for s in starts], axis=0)

    def attend(q, kw, vw, bias):
        q2, kw, vw = q.astype(BF16), kw.astype(BF16), vw.astype(BF16)
        zero = jnp.zeros_like(q2)

        def one_head(qh):
            s = lax.dot_general(qh, kw, nt_dims, preferred_element_type=F32) + bias
            m = jnp.max(s, axis=-1, keepdims=True)
            p = jnp.exp(s - m)
            l = jnp.sum(p, axis=-1, keepdims=True)
            o = jnp.dot(p.astype(BF16), vw, preferred_element_type=F32)
            return m, l, o

        ma, la, oa = one_head(jnp.where(head_a, q2, zero))
        mb, lb, ob = one_head(jnp.where(head_a, zero, q2))
        return jnp.where(head_a, ma, mb), jnp.where(head_a, la, lb), jnp.where(head_a, oa, ob)

    def block(pattern, kind, q_starts, k_starts, bias):
        pieces = len(q_starts)
        q_rows, k_rows = ATT_TQ // pieces, ATT_W // pieces
        q = gather(q_ref.at[0], q_starts, q_rows)
        kw = gather(k_ref.at[0], k_starts, k_rows)
        vw = gather(v_ref.at[0], k_starts, k_rows)
        m_blk, l_blk, o_blk = attend(q, kw, vw, bias)
        if pattern > 0:
            m_old = gather(m_acc, q_starts, q_rows)
            m_new = jnp.maximum(m_old, m_blk)
            w_old = jnp.exp(m_old - m_new)
            w_blk = jnp.exp(m_blk - m_new)
            l_blk = w_old * gather(s_acc, q_starts, q_rows) + w_blk * l_blk
            o_blk = w_old * gather(o_ref, q_starts, q_rows) + w_blk * o_blk
            m_blk = m_new
        for a, s in enumerate(q_starts):
            rows = slice(a * q_rows, (a + 1) * q_rows)
            m_acc[pl.ds(s, q_rows), :] = m_blk[rows]
            s_acc[pl.ds(s, q_rows), :] = l_blk[rows]
            o_ref[pl.ds(s, q_rows), :] = o_blk[rows]

    def aligned(x):
        if isinstance(x, int):
            assert x % SUBLANES == 0
            return x
        return pl.multiple_of(x, SUBLANES)

    for pattern, d in enumerate(DILATIONS):
        pieces = RUNS // d
        q_rows, k_rows = ATT_TQ // pieces, ATT_W // pieces
        n_blk = (S // d) // ATT_TQ
        assert n_blk >= 2, "first and last key blocks must be distinct"
        first_groups = [_block_layout(d, kind)[2] for kind in (INTERIOR, FIRST, LAST)]

        def run_block(sub, jb, kind, pattern=pattern, d=d, pieces=pieces, q_rows=q_rows,
                      first_groups=first_groups):
            g0 = jb * q_rows
            q_starts = [aligned((sub + d * a) * J + g0) for a in range(pieces)]
            if kind == INTERIOR and first_groups[kind] % SUBLANES:
                lead = SUBLANES - (-first_groups[kind]) % SUBLANES
                k_starts = [aligned((sub + d * a) * J + g0 - lead + first_groups[kind]) + lead
                            for a in range(pieces)]
            else:
                k_starts = [aligned((sub + d * a) * J + g0 + first_groups[kind]) for a in range(pieces)]
            block(pattern, kind, q_starts, k_starts, bias_ref[3 * pattern + kind])

        def kind_of(jb, n_blk=n_blk):
            return FIRST if jb == 0 else LAST if jb == n_blk - 1 else INTERIOR

        group = ATT_UNROLL
        if n_blk <= group:
            subs_per_trip = group // n_blk

            def short_subs(t, carry, run_block=run_block, n_blk=n_blk, subs_per_trip=subs_per_trip,
                           kind_of=kind_of):
                for k in range(subs_per_trip):
                    for jb in range(n_blk):
                        run_block(t * subs_per_trip + k, jb, kind_of(jb))
                return carry

            assert d % subs_per_trip == 0
            lax.fori_loop(0, d // subs_per_trip, short_subs, 0)
        else:
            n_mid = n_blk - 2 * group
            assert n_mid >= 0 and n_mid % group == 0

            def long_sub(sub, carry, run_block=run_block, n_blk=n_blk, n_mid=n_mid, kind_of=kind_of,
                         group=group):
                for jb in range(group):
                    run_block(sub, jb, kind_of(jb))

                def middle(t, c):
                    for k in range(group):
                        run_block(sub, group + t * group + k, INTERIOR)
                    return c

                if n_mid:
                    lax.fori_loop(0, n_mid // group, middle, 0)
                for jb in range(n_blk - group, n_blk):
                    run_block(sub, jb, kind_of(jb))
                return carry

            if d == 1:
                long_sub(0, 0)
            else:
                lax.fori_loop(0, d, long_sub, 0)

    def normalise(c, carry):
        rows = pl.ds(pl.multiple_of(c * ATT_W, ATT_W), ATT_W)
        o_ref[rows, :] = o_ref[rows, :] / s_acc[rows, :]
        return carry

    lax.fori_loop(0, S // ATT_W, normalise, 0)


def _attention(proj, bias):
    S = proj.shape[1] * proj.shape[2]
    qkv = proj.reshape(N_SLABS, S, LANES)
    qkv_spec = lambda off: pl.BlockSpec((1, S, LANES), lambda p: (p + off, 0, 0))
    return pl.pallas_call(
        _attention_kernel,
        grid=(N_PAIRS,),
        in_specs=[
            qkv_spec(Q_SLAB), qkv_spec(K_SLAB), qkv_spec(V_SLAB),
            pl.BlockSpec(bias.shape, lambda p: (0, 0, 0)),
        ],
        out_specs=pl.BlockSpec((S, LANES), lambda p: (0, p)),
        out_shape=jax.ShapeDtypeStruct((S, C_B), F32),
        scratch_shapes=[pltpu.VMEM((S, LANES), F32), pltpu.VMEM((S, LANES), F32)],
        compiler_params=pltpu.CompilerParams(
            dimension_semantics=("arbitrary",), vmem_limit_bytes=VMEM_LIMIT),
        name="attention",
    )(qkv, qkv, qkv, bias)


OUT_CAST_ROWS = 256
OUT_VMEM_LIMIT = 58 * 1024 * 1024


def _out_proj_kernel(ya_ref, yb_ref, yc_ref, gb_ref, w_ref, x_ref, gp_ref, o_ref, wb_ref, *by_run,
                     x_is_natural, out_is_natural):
    @pl.when(pl.program_id(0) == 0)
    def _():
        for c in range(MIX_WIDTH // OUT_CAST_ROWS):
            rows = slice(c * OUT_CAST_ROWS, (c + 1) * OUT_CAST_ROWS)
            wb_ref[rows, :] = w_ref[rows, :].astype(BF16)

    yb = _load_rows(yb_ref)
    ybn = (yb * _rms_scale(yb) * gb_ref[...]).astype(BF16)
    acc = jnp.dot(_load_rows(ya_ref), wb_ref[0:C_A, :], preferred_element_type=F32)
    acc = acc + jnp.dot(ybn, wb_ref[C_A:C_A + C_B, :], preferred_element_type=F32)
    acc = acc + jnp.dot(_load_rows(yc_ref), wb_ref[C_A + C_B:, :], preferred_element_type=F32)
    update = acc * _rms_scale(acc) * gp_ref[...]
    if x_is_natural:
        rows = update.shape[0] // RUNS
        _regroup_by_run(x_ref, by_run[0])
        for r in range(RUNS):
            o_ref[r] = by_run[0][r, 0:rows, :] + update[r * rows:(r + 1) * rows]
    elif out_is_natural:
        y = _load_rows(x_ref) + update
        rows = y.shape[0] // RUNS
        for r in range(RUNS):
            o_ref[:, r, :] = y[r * rows:(r + 1) * rows]
    else:
        o_ref[...] = x_ref[...] + update


def _out_proj(ya, yb, yc, g_b, w, layer, x, x_is_natural, out_is_natural, g_post):
    S = ya.shape[0]
    tm = _row_tile(S)
    const2 = lambda i: (0, 0)
    assert not (x_is_natural and out_is_natural)
    mixed = x_is_natural or out_is_natural
    ya_view, ya_spec = _rows_spec(S, C_A, mixed=mixed)
    yb_view, yb_spec = _rows_spec(S, C_B, mixed=mixed)
    yc_view, yc_spec = _rows_spec(S, C_C, mixed=mixed)
    res_view, res_spec = _rows_spec(S, D_MODEL, mixed=mixed)
    x_view, x_spec = _natural_spec(S, D_MODEL) if x_is_natural else (res_view, res_spec)
    out_view, out_spec = _natural_spec(S, D_MODEL) if out_is_natural else (res_view, res_spec)
    out = pl.pallas_call(
        functools.partial(_out_proj_kernel, x_is_natural=x_is_natural, out_is_natural=out_is_natural),
        grid=(S // tm,),
        in_specs=[
            ya_spec, yb_spec, yc_spec,
            pl.BlockSpec((1, C_B), const2),
            pl.BlockSpec((None, MIX_WIDTH, D_MODEL), lambda i: (layer, 0, 0), pipeline_mode=pl.Buffered(1)),
            x_spec,
            pl.BlockSpec((1, D_MODEL), const2),
        ],
        out_specs=out_spec,
        out_shape=jax.ShapeDtypeStruct(out_view, F32),
        scratch_shapes=[pltpu.VMEM((MIX_WIDTH, D_MODEL), BF16)] + (
            [pltpu.VMEM((RUNS, _regroup_pitch(tm // RUNS), D_MODEL), F32)] if x_is_natural else []),
        compiler_params=pltpu.CompilerParams(
            dimension_semantics=("arbitrary",), vmem_limit_bytes=OUT_VMEM_LIMIT),
        name="out_proj",
    )(ya.reshape(ya_view), yb.reshape(yb_view), yc.reshape(yc_view), g_b, w, x.reshape(x_view), g_post)
    return out.reshape(S, D_MODEL)


FFN_TM = 1024
FFN_TH = 256
FFN_VMEM_LIMIT = 58 * 1024 * 1024


def _ffn_kernel(x_ref, gpre_ref, wg_ref, wu_ref, wd_ref, gpost_ref, o_ref, h_ref):
    j = pl.program_id(1)

    def chunk_out(h):
        gate = jnp.dot(h, wg_ref[...].astype(BF16), preferred_element_type=F32)
        up = jnp.dot(h, wu_ref[...].astype(BF16), preferred_element_type=F32)
        act = (gate * jax.nn.sigmoid(gate) * up).astype(BF16)
        return jnp.dot(act, wd_ref[...].astype(BF16), preferred_element_type=F32)

    @pl.when(j == 0)
    def _():
        x = x_ref[...]
        h = (x * _rms_scale(x) * gpre_ref[...]).astype(BF16)
        h_ref[...] = h
        o_ref[...] = chunk_out(h)

    @pl.when(j > 0)
    def _():
        o_ref[...] += chunk_out(h_ref[...])

    @pl.when(j == pl.num_programs(1) - 1)
    def _():
        f = o_ref[...]
        o_ref[...] = x_ref[...] + f * _rms_scale(f) * gpost_ref[...]


def _ffn(x, g_pre, w_gate, w_up, w_down, layer, g_post):
    S = x.shape[0]
    tm = min(FFN_TM, S // 2)
    row_spec = pl.BlockSpec((tm, D_MODEL), lambda i, j: (i, 0))
    return pl.pallas_call(
        _ffn_kernel,
        grid=(S // tm, FFN_HIDDEN // FFN_TH),
        in_specs=[
            row_spec,
            pl.BlockSpec((1, D_MODEL), lambda i, j: (0, 0)),
            pl.BlockSpec((None, D_MODEL, FFN_TH), lambda i, j: (layer, 0, j)),
            pl.BlockSpec((None, D_MODEL, FFN_TH), lambda i, j: (layer, 0, j)),
            pl.BlockSpec((None, FFN_TH, D_MODEL), lambda i, j: (layer, j, 0)),
            pl.BlockSpec((1, D_MODEL), lambda i, j: (0, 0)),
        ],
        out_specs=row_spec,
        out_shape=jax.ShapeDtypeStruct((S, D_MODEL), F32),
        scratch_shapes=[pltpu.VMEM((tm, D_MODEL), BF16)],
        compiler_params=pltpu.CompilerParams(
            dimension_semantics=("arbitrary", "arbitrary"), vmem_limit_bytes=FFN_VMEM_LIMIT),
        name="ffn",
    )(x, g_pre, w_gate, w_up, w_down, g_post)


def _rope_lane_tables(S):
    half = ROT_DIM // 2
    J = S // RUNS
    pos = (np.arange(J)[None, :] * RUNS + np.arange(RUNS)[:, None]).reshape(S).astype(np.float64)
    inv = ROPE_THETA ** (-np.arange(0, ROT_DIM, 2, dtype=np.float64) / ROT_DIM)
    ang = pos[:, None] * inv[None, :]
    cos, sin = np.cos(ang), np.sin(ang)
    ones = np.ones((S, HEAD_DIM - ROT_DIM))
    zeros_h = np.zeros((S, half))
    zeros_r = np.zeros((S, HEAD_DIM - ROT_DIM))
    cos_h = np.concatenate([cos, cos, ones], axis=1)
    sin_lo_h = np.concatenate([-sin, zeros_h, zeros_r], axis=1)
    sin_hi_h = np.concatenate([zeros_h, sin, zeros_r], axis=1)
    pair = lambda t: jnp.asarray(np.concatenate([t, t], axis=1).astype(np.float32))
    return pair(cos_h), pair(sin_lo_h), pair(sin_hi_h)


def kernel(x, w_in, conv_w, conv_b, conv_ln_g, conv_ln_b, pool_w, pool_scale, g_mix, w_out,
           g_pre_mix, g_post_mix, g_pre_ffn, g_post_ffn, w_gate, w_up, w_down):
    B, S, _ = x.shape
    depth = w_in.shape[0]
    assert B == 1, "the sequence-mixing kernels treat all rows as one sequence"
    cos_t, sin_lo_t, sin_hi_t = _rope_lane_tables(S)
    bias = jnp.asarray(_band_bias())
    row = lambda v: v.reshape(1, -1)
    w_in, pool_w = w_in.astype(BF16), pool_w.astype(BF16)

    xb = x.reshape(S, D_MODEL)
    for l in range(depth):
        natural = l == 0
        proj = _in_proj(xb, natural, row(g_pre_mix[l]), w_in, l, cos_t, sin_lo_t, sin_hi_t)
        ya, yc = _local_mix(proj, conv_w[l], row(conv_b[l]), row(conv_ln_g[l]),
                            row(conv_ln_b[l]), pool_w[l], row(pool_scale[l]),
                            row(g_mix[l, :C_A]), row(g_mix[l, C_A + C_B:]))
        yb = _attention(proj, bias)
        xb = _out_proj(ya, yb, yc, row(g_mix[l, C_A:C_A + C_B]), w_out, l, xb, natural, l == depth - 1,
                       row(g_post_mix[l]))
        xb = _ffn(xb, row(g_pre_ffn[l]), w_gate, w_up, w_down, l, row(g_post_ffn[l]))
    return xb.reshape(B, S, D_MODEL)
```
